```python
import math
import jax, jax.numpy as jnp
from jax import lax
import numpy as np

D_MODEL = 1024
BATCH = 8
SEQ = 2048
DEPTH = 1

CHUNK = 64
D_MIX = D_MODEL
D_LRU = D_MIX // 2
D_CONV = D_MIX - D_LRU
LRU_HEADS = 8
LRU_HEAD_DIM = D_LRU // LRU_HEADS
CONV_GROUPS = 8
CONV_GROUP_DIM = D_CONV // CONV_GROUPS
LRU_CONV_WIDTH = 4
SHORT_CONV_WIDTH = 3
D_FF = 4 * D_MODEL
C_GATE = 8.0
MIN_RAD = 0.9
MAX_RAD = 0.999
EPS = 1e-6
N_ADA = 6
D_IN = 2 * D_LRU + 3 * D_CONV

kernel_name = "hybrid_rglru_shortconv_adaln_block"


def rmsnorm(x, g):
    xf = x.astype(jnp.float32)
    y = xf * lax.rsqrt(jnp.mean(xf * xf, axis=-1, keepdims=True) + EPS)
    return (y * g.astype(jnp.float32)).astype(x.dtype)


def headwise_rmsnorm(y, g, n_heads):
    b, s, w = y.shape
    yh = y.reshape(b, s, n_heads, w // n_heads).astype(jnp.float32)
    yh = yh * lax.rsqrt(jnp.mean(yh * yh, axis=-1, keepdims=True) + EPS)
    return (yh.reshape(b, s, w) * g.astype(jnp.float32)).astype(y.dtype)


def causal_depthwise_conv(x, w):
    k, ch = w.shape
    rhs = w[:, None, :].astype(x.dtype)
    return lax.conv_general_dilated(
        x, rhs, window_strides=(1,), padding=[(k - 1, 0)],
        dimension_numbers=("NWC", "WIO", "NWC"), feature_group_count=ch)


def chunked_linear_scan(a, b):
    bn, s, w = a.shape
    nc = s // CHUNK
    a = a.reshape(bn, nc, CHUNK, w)
    b = b.reshape(bn, nc, CHUNK, w)

    def combine(left, right):
        al, bl = left
        ar, br = right
        return al * ar, ar * bl + br

    a_cum, h_loc = lax.associative_scan(combine, (a, b), axis=2)

    def step(h, inp):
        a_last, h_last = inp
        return a_last * h + h_last, h

    _, h_in = lax.scan(step, jnp.zeros((bn, w), jnp.float32),
                       (jnp.swapaxes(a_cum[:, :, -1], 0, 1), jnp.swapaxes(h_loc[:, :, -1], 0, 1)))
    h_in = jnp.swapaxes(h_in, 0, 1)
    h = h_loc + a_cum * h_in[:, :, None, :]
    return h.reshape(bn, s, w)


def rg_lru(xl, gate_a_w, gate_a_b, gate_x_w, gate_x_b, a_param):
    bn, s, w = xl.shape
    xh = xl.reshape(bn, s, LRU_HEADS, LRU_HEAD_DIM)
    r = jax.nn.sigmoid(jnp.einsum("bshi,hij->bshj", xh, gate_a_w).reshape(bn, s, w) + gate_a_b)
    i = jax.nn.sigmoid(jnp.einsum("bshi,hij->bshj", xh, gate_x_w).reshape(bn, s, w) + gate_x_b)
    log_a = -C_GATE * r.astype(jnp.float32) * jax.nn.softplus(a_param.astype(jnp.float32))
    a = jnp.exp(log_a)
    mult = jnp.sqrt(-jnp.expm1(2.0 * log_a))
    is_first = (jnp.arange(s) == 0)[None, :, None]
    mult = jnp.where(is_first, jnp.ones_like(mult), mult)
    bx = mult * (i * xl).astype(jnp.float32)
    return chunked_linear_scan(a, bx).astype(xl.dtype)


def setup_inputs(seed: int = 0) -> dict:
    key = jax.random.key(seed)
    ks = jax.random.split(key, 24)
    f32 = jnp.float32
    nrm = lambda k, shape, scale: (jax.random.normal(k, shape, f32) * scale)
    x = jax.random.normal(ks[0], (BATCH, SEQ, D_MODEL), f32)
    c = jax.random.normal(ks[1], (BATCH, D_MODEL), f32)
    ada_w = nrm(ks[2], (DEPTH, D_MODEL, N_ADA * D_MODEL), 0.5 * D_MODEL ** -0.5)
    ada_b = nrm(ks[3], (DEPTH, N_ADA * D_MODEL), 0.01)
    norm1_g = 1.0 + nrm(ks[4], (DEPTH, D_MODEL), 0.02)
    w_in = nrm(ks[5], (DEPTH, D_MODEL, D_IN), D_MODEL ** -0.5)
    lru_conv_w = nrm(ks[6], (DEPTH, LRU_CONV_WIDTH, D_LRU), LRU_CONV_WIDTH ** -0.5)
    lru_conv_b = nrm(ks[7], (DEPTH, D_LRU), 0.01)
    gate_a_w = nrm(ks[8], (DEPTH, LRU_HEADS, LRU_HEAD_DIM, LRU_HEAD_DIM), LRU_HEAD_DIM ** -0.5)
    gate_a_b = nrm(ks[9], (DEPTH, D_LRU), 0.01)
    gate_x_w = nrm(ks[10], (DEPTH, LRU_HEADS, LRU_HEAD_DIM, LRU_HEAD_DIM), LRU_HEAD_DIM ** -0.5)
    gate_x_b = nrm(ks[11], (DEPTH, D_LRU), 0.01)
    u = jax.random.uniform(ks[12], (DEPTH, D_LRU), f32, MIN_RAD ** 2, MAX_RAD ** 2)
    a_param = jnp.log(jnp.expm1(-0.5 * jnp.log(u)))
    short_conv_w = nrm(ks[13], (DEPTH, SHORT_CONV_WIDTH, D_CONV), SHORT_CONV_WIDTH ** -0.5)
    lru_out_g = 1.0 + nrm(ks[14], (DEPTH, D_LRU), 0.02)
    conv_out_g = 1.0 + nrm(ks[15], (DEPTH, D_CONV), 0.02)
    w_out = nrm(ks[16], (DEPTH, D_MIX, D_MODEL), D_MIX ** -0.5)
    norm2_g = 1.0 + nrm(ks[17], (DEPTH, D_MODEL), 0.02)
    w_mlp1 = nrm(ks[18], (DEPTH, D_MODEL, D_FF), D_MODEL ** -0.5)
    w_mlp2 = nrm(ks[19], (DEPTH, D_FF, D_MODEL), D_FF ** -0.5)
    final_g = 1.0 + nrm(ks[20], (D_MODEL,), 0.02)
    return {"x": x, "c": c, "ada_w": ada_w, "ada_b": ada_b, "norm1_g": norm1_g,
            "w_in": w_in, "lru_conv_w": lru_conv_w, "lru_conv_b": lru_conv_b,
            "gate_a_w": gate_a_w, "gate_a_b": gate_a_b, "gate_x_w": gate_x_w,
            "gate_x_b": gate_x_b, "a_param": a_param, "short_conv_w": short_conv_w,
            "lru_out_g": lru_out_g, "conv_out_g": conv_out_g, "w_out": w_out,
            "norm2_g": norm2_g, "w_mlp1": w_mlp1, "w_mlp2": w_mlp2, "final_g": final_g}


def reference(x, c, ada_w, ada_b, norm1_g, w_in, lru_conv_w, lru_conv_b, gate_a_w, gate_a_b,
              gate_x_w, gate_x_b, a_param, short_conv_w, lru_out_g, conv_out_g, w_out,
              norm2_g, w_mlp1, w_mlp2, final_g):
    sc = jax.nn.silu(c)
    for l in range(DEPTH):
        mod = sc @ ada_w[l] + ada_b[l]
        shift1, scale1, gate1, shift2, scale2, gate2 = jnp.split(mod[:, None, :], N_ADA, axis=-1)

        h = rmsnorm(x, norm1_g[l]) * (1.0 + scale1) + shift1
        proj = h @ w_in[l]
        u_lx, u_ly, u_b, u_c, u_v = jnp.split(
            proj, np.cumsum([D_LRU, D_LRU, D_CONV, D_CONV])[:].tolist(), axis=-1)

        xl = causal_depthwise_conv(u_lx, lru_conv_w[l]) + lru_conv_b[l]
        hl = rg_lru(xl, gate_a_w[l], gate_a_b[l], gate_x_w[l], gate_x_b[l], a_param[l])
        y_lru = headwise_rmsnorm(jax.nn.gelu(u_ly) * hl, lru_out_g[l], LRU_HEADS)

        y_conv = u_b * causal_depthwise_conv(u_c * u_v, short_conv_w[l])
        y_conv = headwise_rmsnorm(y_conv, conv_out_g[l], CONV_GROUPS)

        mixed = jnp.concatenate([y_lru, y_conv], axis=-1) @ w_out[l]
        x = x + gate1 * mixed

        h2 = rmsnorm(x, norm2_g[l]) * (1.0 + scale2) + shift2
        x = x + gate2 * (jnp.square(jax.nn.relu(h2 @ w_mlp1[l])) @ w_mlp2[l])
    return rmsnorm(x, final_g)
```

```python
import functools

import jax
import jax.numpy as jnp
from jax import lax
from jax.experimental import pallas as pl
from jax.experimental.pallas import tpu as pltpu

D_MODEL = 1024
D_LRU = 512
D_CONV = 512
N_HEADS = 8
HEAD_DIM = 64
D_FF = 4096
D_IN = 2 * D_LRU + 3 * D_CONV
N_ADA = 6
C_GATE = 8.0
EPS = 1e-6

LANES = 128
SUBLANES = 8
MXU_DIM = 256
HEADS_PER_MXU_TILE = MXU_DIM // HEAD_DIM

TIME_CHUNK = 64
FF_CHUNK = 1024
ADA_BLOCK = 512
SCAN_PITCH = TIME_CHUNK + SUBLANES
N_SLABS = D_LRU // LANES

VMEM_LIMIT_BYTES = 56 * 1024 * 1024


def _ada_kernel(c_ref, w_ref, b_ref, o_ref):
    sc = jax.nn.silu(c_ref[...]).astype(jnp.bfloat16)
    o_ref[...] = jnp.dot(sc, w_ref[...].astype(jnp.bfloat16),
                         preferred_element_type=jnp.float32) + b_ref[...]


def _ada_modulation(c, ada_w, ada_b):
    bsz, d = c.shape
    n = ada_w.shape[1]
    return pl.pallas_call(
        _ada_kernel,
        grid=(n // ADA_BLOCK,),
        in_specs=[
            pl.BlockSpec((bsz, d), lambda j: (0, 0)),
            pl.BlockSpec((d, ADA_BLOCK), lambda j: (0, j)),
            pl.BlockSpec((1, ADA_BLOCK), lambda j: (0, j)),
        ],
        out_specs=pl.BlockSpec((bsz, ADA_BLOCK), lambda j: (0, j)),
        out_shape=jax.ShapeDtypeStruct((bsz, n), jnp.float32),
        name="ada_modulation",
    )(c, ada_w, ada_b.reshape(1, n))


def _rms_scale(v):
    return lax.rsqrt(jnp.mean(v * v, axis=-1, keepdims=True) + EPS)


def _time_shift(u, prev_tail, j):
    rolled = pltpu.roll(u, j, axis=1)
    rolled_tail = pltpu.roll(prev_tail, j, axis=1)
    t_idx = lax.broadcasted_iota(jnp.int32, (1, SUBLANES, 1), 1)
    head = jnp.where(t_idx < j, rolled_tail, rolled[:, :SUBLANES, :])
    return jnp.concatenate([head, rolled[:, SUBLANES:, :]], axis=1)


def _head_mean_square(y2d, pool):
    sq = (y2d * y2d).astype(jnp.bfloat16)
    halves = [
        jnp.dot(sq[:, k * MXU_DIM:(k + 1) * MXU_DIM], pool, preferred_element_type=jnp.float32)
        for k in range(sq.shape[1] // MXU_DIM)
    ]
    return jnp.concatenate(halves, axis=1)


def _block_kernel(x_ref, mod_ref, n1g_ref, w_in_ref, lcw_ref, lcb_ref, wg_ref, gab_ref, gxb_ref,
                  ap_ref, scw_ref, log_ref, cog_ref, pool_ref, w_out_ref, n2g_ref, w1_ref, w2_ref,
                  fg_ref, o_ref,
                  lx_tail, cv_tail, h_state, a_slab, b_slab, hid_ref):
    step = pl.program_id(0)
    bsz, tc, d = x_ref.shape
    rows = bsz * tc

    @pl.when(step == 0)
    def _():
        lx_tail[...] = jnp.zeros_like(lx_tail)
        cv_tail[...] = jnp.zeros_like(cv_tail)
        h_state[...] = jnp.zeros_like(h_state)

    shift1, scale1, gate1, shift2, scale2, gate2 = [mod_ref[k] for k in range(N_ADA)]

    x = x_ref[...]
    h = x * _rms_scale(x) * (n1g_ref[...][None] * (1.0 + scale1)) + shift1
    proj = jnp.dot(h.reshape(rows, d).astype(jnp.bfloat16), w_in_ref[...],
                   preferred_element_type=jnp.float32)

    def piece(k, width):
        return proj[:, k:k + width].reshape(bsz, tc, width)

    u_lx = piece(0, D_LRU)
    u_ly = piece(D_LRU, D_LRU)
    u_b = piece(2 * D_LRU, D_CONV)
    u_c = piece(2 * D_LRU + D_CONV, D_CONV)
    u_v = piece(2 * D_LRU + 2 * D_CONV, D_CONV)

    lcw = lcw_ref[...]
    prev_lx = lx_tail[...]
    xl = u_lx * lcw[3][None, None, :] + lcb_ref[...][None]
    for j in range(1, 4):
        xl = xl + _time_shift(u_lx, prev_lx, j) * lcw[3 - j][None, None, :]
    lx_tail[...] = u_lx[:, tc - SUBLANES:, :]

    xl2 = xl.reshape(rows, D_LRU).astype(jnp.bfloat16)
    gates = [
        jnp.dot(xl2[:, k * MXU_DIM:(k + 1) * MXU_DIM], wg_ref[k],
                preferred_element_type=jnp.float32)
        for k in range(D_LRU // MXU_DIM)
    ]
    ga = jnp.concatenate([g[:, :MXU_DIM] for g in gates], axis=1).reshape(bsz, tc, D_LRU)
    gx = jnp.concatenate([g[:, MXU_DIM:] for g in gates], axis=1).reshape(bsz, tc, D_LRU)
    r = jax.nn.sigmoid(ga + gab_ref[...][None])
    i = jax.nn.sigmoid(gx + gxb_ref[...][None])
    ap = ap_ref[...]
    softplus = jnp.maximum(ap, 0.0) + jnp.log1p(jnp.exp(-jnp.abs(ap)))
    log_a = r * (-C_GATE * softplus)[None]
    a = jnp.exp(log_a)
    mult = jnp.sqrt(1.0 - a * a)
    t_idx = lax.broadcasted_iota(jnp.int32, (1, tc, 1), 1)
    mult = jnp.where((t_idx == 0) & (step == 0), 1.0, mult)
    bx = mult * (i * xl)

    for b in range(bsz):
        for s in range(N_SLABS):
            rows_b = pl.ds(b * SCAN_PITCH, tc)
            a_slab[s, rows_b, :] = a[b, :, s * LANES:(s + 1) * LANES]
            b_slab[s, rows_b, :] = bx[b, :, s * LANES:(s + 1) * LANES]

    def scan_step(t, carry):
        new = []
        for s in range(N_SLABS):
            idx = pl.ds(t, bsz, stride=SCAN_PITCH)
            h_t = a_slab[s, idx, :] * carry[s] + b_slab[s, idx, :]
            b_slab[s, idx, :] = h_t
            new.append(h_t)
        return tuple(new)

    h0 = tuple(h_state[:, s * LANES:(s + 1) * LANES] for s in range(N_SLABS))
    h_last = lax.fori_loop(0, tc, scan_step, h0)
    for s in range(N_SLABS):
        h_state[:, s * LANES:(s + 1) * LANES] = h_last[s]

    hl = jnp.stack([
        jnp.concatenate([b_slab[s, pl.ds(b * SCAN_PITCH, tc), :] for s in range(N_SLABS)], axis=1)
        for b in range(bsz)], axis=0)

    pool = pool_ref[...]
    y_l = (jax.nn.gelu(u_ly) * hl).reshape(rows, D_LRU)
    y_l = y_l * lax.rsqrt(_head_mean_square(y_l, pool) + EPS) * log_ref[...]

    scw = scw_ref[...]
    cv = u_c * u_v
    prev_cv = cv_tail[...]
    conv = cv * scw[2][None, None, :]
    for j in range(1, 3):
        conv = conv + _time_shift(cv, prev_cv, j) * scw[2 - j][None, None, :]
    cv_tail[...] = cv[:, tc - SUBLANES:, :]
    y_c = (u_b * conv).reshape(rows, D_CONV)
    y_c = y_c * lax.rsqrt(_head_mean_square(y_c, pool) + EPS) * cog_ref[...]

    mixed = jnp.dot(jnp.concatenate([y_l, y_c], axis=1).astype(jnp.bfloat16), w_out_ref[...],
                    preferred_element_type=jnp.float32)
    x1 = x + gate1 * mixed.reshape(bsz, tc, d)

    h2 = x1 * _rms_scale(x1) * (n2g_ref[...][None] * (1.0 + scale2)) + shift2
    h2 = h2.reshape(rows, d).astype(jnp.bfloat16)
    for k in range(D_FF // FF_CHUNK):
        cols = slice(k * FF_CHUNK, (k + 1) * FF_CHUNK)
        hk = jnp.dot(h2, w1_ref[:, cols], preferred_element_type=jnp.float32)
        hid_ref[:, cols] = jnp.square(jnp.maximum(hk, 0.0)).astype(jnp.bfloat16)
    mlp = jnp.dot(hid_ref[...], w2_ref[...], preferred_element_type=jnp.float32)
    x2 = x1 + gate2 * mlp.reshape(bsz, tc, d)
    o_ref[...] = x2 * _rms_scale(x2) * fg_ref[...][None]


def _block_diag(w):
    n, i, j = w.shape
    eye = jnp.eye(n, dtype=w.dtype)
    return (eye[:, None, :, None] * w[:, :, None, :]).reshape(n * i, n * j)


def _gate_weights(gate_a_w, gate_x_w):
    tiles = []
    for k in range(N_HEADS // HEADS_PER_MXU_TILE):
        hs = slice(k * HEADS_PER_MXU_TILE, (k + 1) * HEADS_PER_MXU_TILE)
        tiles.append(jnp.concatenate([_block_diag(gate_a_w[hs]), _block_diag(gate_x_w[hs])], axis=1))
    return jnp.stack(tiles).astype(jnp.bfloat16)


def _const_spec(shape):
    return pl.BlockSpec(shape, lambda s: (0,) * len(shape), pipeline_mode=pl.Buffered(1))


@jax.jit
def kernel(x, c, ada_w, ada_b, norm1_g, w_in, lru_conv_w, lru_conv_b, gate_a_w, gate_a_b, gate_x_w,
           gate_x_b, a_param, short_conv_w, lru_out_g, conv_out_g, w_out, norm2_g, w_mlp1, w_mlp2,
           final_g):
    bsz, seq, d = x.shape
    assert (d, seq % TIME_CHUNK, bsz) == (D_MODEL, 0, SUBLANES)
    assert ada_w.shape[0] == 1, "one layer"
    bf16 = jnp.bfloat16
    row = lambda v: v.reshape(1, -1)

    mod = _ada_modulation(c, ada_w[0], ada_b[0])
    mod = mod.reshape(bsz, N_ADA, 1, d).transpose(1, 0, 2, 3)
    pool = _block_diag(jnp.full((HEADS_PER_MXU_TILE, HEAD_DIM, HEAD_DIM), 1.0 / HEAD_DIM, bf16))

    operands = [
        (mod, (N_ADA, bsz, 1, d)),
        (row(norm1_g[0]), (1, d)),
        (w_in[0].astype(bf16), (d, D_IN)),
        (lru_conv_w[0], (4, D_LRU)),
        (row(lru_conv_b[0]), (1, D_LRU)),
        (_gate_weights(gate_a_w[0], gate_x_w[0]), (D_LRU // MXU_DIM, MXU_DIM, 2 * MXU_DIM)),
        (row(gate_a_b[0]), (1, D_LRU)),
        (row(gate_x_b[0]), (1, D_LRU)),
        (row(a_param[0]), (1, D_LRU)),
        (short_conv_w[0], (3, D_CONV)),
        (row(lru_out_g[0]), (1, D_LRU)),
        (row(conv_out_g[0]), (1, D_CONV)),
        (pool, (MXU_DIM, MXU_DIM)),
        (w_out[0].astype(bf16), (D_LRU + D_CONV, d)),
        (row(norm2_g[0]), (1, d)),
        (w_mlp1[0].astype(bf16), (d, D_FF)),
        (w_mlp2[0].astype(bf16), (D_FF, d)),
        (row(final_g), (1, d)),
    ]
    x_spec = pl.BlockSpec((bsz, TIME_CHUNK, d), lambda s: (0, s, 0))
    rows = bsz * TIME_CHUNK
    return pl.pallas_call(
        _block_kernel,
        grid=(seq // TIME_CHUNK,),
        in_specs=[x_spec] + [_const_spec(shape) for _, shape in operands],
        out_specs=x_spec,
        out_shape=jax.ShapeDtypeStruct(x.shape, x.dtype),
        scratch_shapes=[
            pltpu.VMEM((bsz, SUBLANES, D_LRU), jnp.float32),
            pltpu.VMEM((bsz, SUBLANES, D_CONV), jnp.float32),
            pltpu.VMEM((bsz, D_LRU), jnp.float32),
            pltpu.VMEM((N_SLABS, bsz * SCAN_PITCH, LANES), jnp.float32),
            pltpu.VMEM((N_SLABS, bsz * SCAN_PITCH, LANES), jnp.float32),
            pltpu.VMEM((rows, D_FF), jnp.bfloat16),
        ],
        compiler_params=pltpu.CompilerParams(
            dimension_semantics=("arbitrary",), vmem_limit_bytes=VMEM_LIMIT_BYTES),
        name="hybrid_block",
    )(x, *[v for v, _ in operands])
```

```python
import jax
import jax.numpy as jnp
from jax import lax
from jax.experimental import pallas as pl
from jax.experimental.pallas import tpu as pltpu

D_MODEL = 1024
D_LRU = 512
D_CONV = 512
N_HEADS = 8
HEAD_DIM = 64
D_FF = 4096
D_IN = 2 * D_LRU + 3 * D_CONV
N_ADA = 6
C_GATE = 8.0
EPS = 1e-6

LANES = 128
SUBLANES = 8
MXU_DIM = 256
HEADS_PER_MXU_TILE = MXU_DIM // HEAD_DIM

TIME_CHUNK = 64
FF_CHUNK = 1024
ADA_BLOCK = 512
SCAN_PITCH = TIME_CHUNK + SUBLANES
N_SLABS = D_LRU // LANES

VMEM_LIMIT_BYTES = 56 * 1024 * 1024


def _ada_kernel(c_ref, w_ref, b_ref, o_ref):
    sc = jax.nn.silu(c_ref[...]).astype(jnp.bfloat16)
    o_ref[...] = jnp.dot(sc, w_ref[...].astype(jnp.bfloat16),
                         preferred_element_type=jnp.float32) + b_ref[...]


def _ada_modulation(c, ada_w, ada_b):
    bsz, d = c.shape
    n = ada_w.shape[1]
    return pl.pallas_call(
        _ada_kernel,
        grid=(n // ADA_BLOCK,),
        in_specs=[
            pl.BlockSpec((bsz, d), lambda j: (0, 0)),
            pl.BlockSpec((d, ADA_BLOCK), lambda j: (0, j)),
            pl.BlockSpec((1, ADA_BLOCK), lambda j: (0, j)),
        ],
        out_specs=pl.BlockSpec((bsz, ADA_BLOCK), lambda j: (0, j)),
        out_shape=jax.ShapeDtypeStruct((bsz, n), jnp.float32),
        name="ada_modulation",
    )(c, ada_w, ada_b.reshape(1, n))


def _rms_scale(v):
    return lax.rsqrt(jnp.mean(v * v, axis=-1, keepdims=True) + EPS)


def _time_shift(u, prev_tail, j):
    rolled = pltpu.roll(u, j, axis=1)
    rolled_tail = pltpu.roll(prev_tail, j, axis=1)
    t_idx = lax.broadcasted_iota(jnp.int32, (1, SUBLANES, 1), 1)
    head = jnp.where(t_idx < j, rolled_tail, rolled[:, :SUBLANES, :])
    return jnp.concatenate([head, rolled[:, SUBLANES:, :]], axis=1)


def _head_mean(sq_ref, pool):
    halves = [
        jnp.dot(sq_ref[:, k * MXU_DIM:(k + 1) * MXU_DIM], pool, preferred_element_type=jnp.float32)
        for k in range(sq_ref.shape[1] // MXU_DIM)
    ]
    return jnp.concatenate(halves, axis=1)


def _block_kernel(x_ref, mod_ref, n1g_ref, w_in_ref, lcw_ref, lcb_ref, wg_ref, gab_ref, gxb_ref,
                  ap_ref, scw_ref, log_ref, cog_ref, pool_ref, w_out_ref, n2g_ref, w1_ref, w2_ref,
                  fg_ref, o_ref,
                  lx_tail, cv_tail, h_state, a_slab, b_slab, hb_ref, x1_ref, h2_ref, hid_ref, mlp_ref,
                  xlb_ref, sql_ref, sqc_ref, yb_ref):
    step = pl.program_id(0)
    bsz, tc, d = x_ref.shape
    rows = bsz * tc
    f32, bf16 = jnp.float32, jnp.bfloat16

    @pl.when(step == 0)
    def _():
        lx_tail[...] = jnp.zeros_like(lx_tail)
        cv_tail[...] = jnp.zeros_like(cv_tail)
        h_state[...] = jnp.zeros_like(h_state)
        x1_ref[...] = jnp.zeros_like(x1_ref)
        hid_ref[...] = jnp.zeros_like(hid_ref)

    shift1, scale1, gate1, shift2, scale2, gate2 = [mod_ref[k] for k in range(N_ADA)]

    def mlp_out_piece(k):
        cols = slice(k * MXU_DIM, (k + 1) * MXU_DIM)
        mlp_ref[:, cols] = jnp.dot(hid_ref[...], w2_ref[:, cols], preferred_element_type=f32)

    def proj(k, width):
        u = jnp.dot(hb_ref[...], w_in_ref[:, k:k + width], preferred_element_type=f32)
        return u.reshape(bsz, tc, width)

    mlp_out_piece(0)
    x = x_ref[...]
    h = x * _rms_scale(x) * (n1g_ref[...][None] * (1.0 + scale1)) + shift1
    hb_ref[...] = h.reshape(rows, d).astype(bf16)

    u_lx = proj(0, D_LRU)
    u_c = proj(2 * D_LRU + D_CONV, D_CONV)
    u_v = proj(2 * D_LRU + 2 * D_CONV, D_CONV)
    lcw = lcw_ref[...]
    prev_lx = lx_tail[...]
    xl = u_lx * lcw[3][None, None, :] + lcb_ref[...][None]
    for j in range(1, 4):
        xl = xl + _time_shift(u_lx, prev_lx, j) * lcw[3 - j][None, None, :]
    lx_tail[...] = u_lx[:, tc - SUBLANES:, :]
    xlb_ref[...] = xl.reshape(rows, D_LRU).astype(bf16)
    scw = scw_ref[...]
    cv = u_c * u_v
    prev_cv = cv_tail[...]
    conv = cv * scw[2][None, None, :]
    for j in range(1, 3):
        conv = conv + _time_shift(cv, prev_cv, j) * scw[2 - j][None, None, :]
    cv_tail[...] = cv[:, tc - SUBLANES:, :]

    gates = [
        jnp.dot(xlb_ref[:, k * MXU_DIM:(k + 1) * MXU_DIM], wg_ref[k], preferred_element_type=f32)
        for k in range(D_LRU // MXU_DIM)
    ]

    u_b = proj(2 * D_LRU, D_CONV)
    u_ly = proj(D_LRU, D_LRU)
    ga = jnp.concatenate([g[:, :MXU_DIM] for g in gates], axis=1).reshape(bsz, tc, D_LRU)
    gx = jnp.concatenate([g[:, MXU_DIM:] for g in gates], axis=1).reshape(bsz, tc, D_LRU)
    r = jax.nn.sigmoid(ga + gab_ref[...][None])
    i = jax.nn.sigmoid(gx + gxb_ref[...][None])
    ap = ap_ref[...]
    softplus = jnp.maximum(ap, 0.0) + jnp.log1p(jnp.exp(-jnp.abs(ap)))
    log_a = r * (-C_GATE * softplus)[None]
    a = jnp.exp(log_a)
    mult = jnp.sqrt(1.0 - a * a)
    t_idx = lax.broadcasted_iota(jnp.int32, (1, SUBLANES, 1), 1)
    mult_head = jnp.where((t_idx == 0) & (step == 0), 1.0, mult[:, :SUBLANES, :])
    mult = jnp.concatenate([mult_head, mult[:, SUBLANES:, :]], axis=1)
    bx = mult * (i * xl)
    for b in range(bsz):
        for s in range(N_SLABS):
            rows_b = pl.ds(b * SCAN_PITCH, tc)
            a_slab[s, rows_b, :] = a[b, :, s * LANES:(s + 1) * LANES]
            b_slab[s, rows_b, :] = bx[b, :, s * LANES:(s + 1) * LANES]

    mlp_out_piece(1)
    y_c = (u_b * conv).reshape(rows, D_CONV)
    sqc_ref[...] = (y_c * y_c).astype(bf16)
    gelu_ly = jax.nn.gelu(u_ly)
    pool = pool_ref[...]
    y_c = y_c * lax.rsqrt(_head_mean(sqc_ref, pool) + EPS) * cog_ref[...]
    yb_ref[:, D_LRU:] = y_c.astype(bf16)

    mlp_out_piece(2)

    def scan_step(t, carry):
        new = []
        for s in range(N_SLABS):
            idx = pl.ds(t, bsz, stride=SCAN_PITCH)
            h_t = a_slab[s, idx, :] * carry[s] + b_slab[s, idx, :]
            b_slab[s, idx, :] = h_t
            new.append(h_t)
        return tuple(new)

    h0 = tuple(h_state[:, s * LANES:(s + 1) * LANES] for s in range(N_SLABS))
    h_last = lax.fori_loop(0, tc, scan_step, h0, unroll=True)
    for s in range(N_SLABS):
        h_state[:, s * LANES:(s + 1) * LANES] = h_last[s]
    hl = jnp.stack([
        jnp.concatenate([b_slab[s, pl.ds(b * SCAN_PITCH, tc), :] for s in range(N_SLABS)], axis=1)
        for b in range(bsz)], axis=0)
    y_l = (gelu_ly * hl).reshape(rows, D_LRU)
    sql_ref[...] = (y_l * y_l).astype(bf16)
    y_l = y_l * lax.rsqrt(_head_mean(sql_ref, pool) + EPS) * log_ref[...]
    yb_ref[:, :D_LRU] = y_l.astype(bf16)

    mixed = jnp.dot(yb_ref[...], w_out_ref[...], preferred_element_type=f32)
    mlp_out_piece(3)
    x1_prev = x1_ref[...]
    x1 = x_ref[...] + gate1 * mixed.reshape(bsz, tc, d)
    x1_ref[...] = x1
    h2 = x1 * _rms_scale(x1) * (n2g_ref[...][None] * (1.0 + scale2)) + shift2
    h2_ref[...] = h2.reshape(rows, d).astype(bf16)

    for k in range(D_FF // FF_CHUNK):
        cols = slice(k * FF_CHUNK, (k + 1) * FF_CHUNK)
        hk = jnp.dot(h2_ref[...], w1_ref[:, cols], preferred_element_type=f32)
        hid_ref[:, cols] = jnp.square(jnp.maximum(hk, 0.0)).astype(bf16)
        if k == 0:
            x2 = x1_prev + gate2 * mlp_ref[...].reshape(bsz, tc, d)
            o_ref[...] = x2 * _rms_scale(x2) * fg_ref[...][None]


def _block_diag(w):
    n, i, j = w.shape
    eye = jnp.eye(n, dtype=w.dtype)
    return (eye[:, None, :, None] * w[:, :, None, :]).reshape(n * i, n * j)


def _gate_weights(gate_a_w, gate_x_w):
    tiles = []
    for k in range(N_HEADS // HEADS_PER_MXU_TILE):
        hs = slice(k * HEADS_PER_MXU_TILE, (k + 1) * HEADS_PER_MXU_TILE)
        tiles.append(jnp.concatenate([_block_diag(gate_a_w[hs]), _block_diag(gate_x_w[hs])], axis=1))
    return jnp.stack(tiles).astype(jnp.bfloat16)


def _const_spec(shape):
    return pl.BlockSpec(shape, lambda s: (0,) * len(shape), pipeline_mode=pl.Buffered(1))


@jax.jit
def kernel(x, c, ada_w, ada_b, norm1_g, w_in, lru_conv_w, lru_conv_b, gate_a_w, gate_a_b, gate_x_w,
           gate_x_b, a_param, short_conv_w, lru_out_g, conv_out_g, w_out, norm2_g, w_mlp1, w_mlp2,
           final_g):
    bsz, seq, d = x.shape
    assert (d, seq % TIME_CHUNK, bsz) == (D_MODEL, 0, SUBLANES)
    assert ada_w.shape[0] == 1, "one layer"
    bf16 = jnp.bfloat16
    row = lambda v: v.reshape(1, -1)

    mod = _ada_modulation(c, ada_w[0], ada_b[0])
    mod = mod.reshape(bsz, N_ADA, 1, d).transpose(1, 0, 2, 3)
    pool = _block_diag(jnp.full((HEADS_PER_MXU_TILE, HEAD_DIM, HEAD_DIM), 1.0 / HEAD_DIM, bf16))

    operands = [
        (mod, (N_ADA, bsz, 1, d)),
        (row(norm1_g[0]), (1, d)),
        (w_in[0].astype(bf16), (d, D_IN)),
        (lru_conv_w[0], (4, D_LRU)),
        (row(lru_conv_b[0]), (1, D_LRU)),
        (_gate_weights(gate_a_w[0], gate_x_w[0]), (D_LRU // MXU_DIM, MXU_DIM, 2 * MXU_DIM)),
        (row(gate_a_b[0]), (1, D_LRU)),
        (row(gate_x_b[0]), (1, D_LRU)),
        (row(a_param[0]), (1, D_LRU)),
        (short_conv_w[0], (3, D_CONV)),
        (row(lru_out_g[0]), (1, D_LRU)),
        (row(conv_out_g[0]), (1, D_CONV)),
        (pool, (MXU_DIM, MXU_DIM)),
        (w_out[0].astype(bf16), (D_LRU + D_CONV, d)),
        (row(norm2_g[0]), (1, d)),
        (w_mlp1[0].astype(bf16), (d, D_FF)),
        (w_mlp2[0].astype(bf16), (D_FF, d)),
        (row(final_g), (1, d)),
    ]
    n_chunks = seq // TIME_CHUNK
    block = (bsz, TIME_CHUNK, d)
    rows = bsz * TIME_CHUNK
    return pl.pallas_call(
        _block_kernel,
        grid=(n_chunks + 1,),
        in_specs=[pl.BlockSpec(block, lambda s: (0, jnp.minimum(s, n_chunks - 1), 0))]
        + [_const_spec(shape) for _, shape in operands],
        out_specs=pl.BlockSpec(block, lambda s: (0, jnp.maximum(s - 1, 0), 0)),
        out_shape=jax.ShapeDtypeStruct(x.shape, x.dtype),
        scratch_shapes=[
            pltpu.VMEM((bsz, SUBLANES, D_LRU), jnp.float32),
            pltpu.VMEM((bsz, SUBLANES, D_CONV), jnp.float32),
            pltpu.VMEM((bsz, D_LRU), jnp.float32),
            pltpu.VMEM((N_SLABS, bsz * SCAN_PITCH, LANES), jnp.float32),
            pltpu.VMEM((N_SLABS, bsz * SCAN_PITCH, LANES), jnp.float32),
            pltpu.VMEM((rows, d), jnp.bfloat16),
            pltpu.VMEM(block, jnp.float32),
            pltpu.VMEM((rows, d), jnp.bfloat16),
            pltpu.VMEM((rows, D_FF), jnp.bfloat16),
            pltpu.VMEM((rows, d), jnp.float32),
            pltpu.VMEM((rows, D_LRU), jnp.bfloat16),
            pltpu.VMEM((rows, D_LRU), jnp.bfloat16),
            pltpu.VMEM((rows, D_CONV), jnp.bfloat16),
            pltpu.VMEM((rows, D_LRU + D_CONV), jnp.bfloat16),
        ],
        compiler_params=pltpu.CompilerParams(
            dimension_semantics=("arbitrary",), vmem_limit_bytes=VMEM_LIMIT_BYTES),
        name="hybrid_block",
    )(x, *[v for v, _ in operands])
```

```python
import jax
import jax.numpy as jnp
from jax import lax
from jax.experimental import pallas as pl
from jax.experimental.pallas import tpu as pltpu

D_MODEL = 1024
D_LRU = 512
D_CONV = 512
N_HEADS = 8
HEAD_DIM = 64
D_FF = 4096
D_IN = 2 * D_LRU + 3 * D_CONV
N_ADA = 6
C_GATE = 8.0
EPS = 1e-6

LANES = 128
SUBLANES = 8
MXU_DIM = 256
HEADS_PER_MXU_TILE = MXU_DIM // HEAD_DIM

TIME_CHUNK = 64
FF_CHUNK = 1024
ADA_BLOCK = 1024
STAGE_ROWS, STAGE_COLS = 256, 1024
SCAN_PITCH = TIME_CHUNK + SUBLANES
N_SLABS = D_LRU // LANES

VMEM_LIMIT_BYTES = 56 * 1024 * 1024


def _ada_kernel(c_ref, w_ref, b_ref, o_ref):
    sc = jax.nn.silu(c_ref[...]).astype(jnp.bfloat16)
    o_ref[...] = jnp.dot(sc, w_ref[...].astype(jnp.bfloat16),
                         preferred_element_type=jnp.float32) + b_ref[...]


def _ada_modulation(c, ada_w, ada_b):
    bsz, d = c.shape
    n = ada_w.shape[1]
    return pl.pallas_call(
        _ada_kernel,
        grid=(n // ADA_BLOCK,),
        in_specs=[
            pl.BlockSpec((bsz, d), lambda j: (0, 0)),
            pl.BlockSpec((d, ADA_BLOCK), lambda j: (0, j)),
            pl.BlockSpec((1, ADA_BLOCK), lambda j: (0, j)),
        ],
        out_specs=pl.BlockSpec((bsz, ADA_BLOCK), lambda j: (0, j)),
        out_shape=jax.ShapeDtypeStruct((bsz, n), jnp.float32),
        name="ada_modulation",
    )(c, ada_w, ada_b.reshape(1, n))


def _rms_scale(v):
    return lax.rsqrt(jnp.mean(v * v, axis=-1, keepdims=True) + EPS)


def _time_shift(u, prev_tail, j):
    rolled = pltpu.roll(u, j, axis=1)
    rolled_tail = pltpu.roll(prev_tail, j, axis=1)
    t_idx = lax.broadcasted_iota(jnp.int32, (1, SUBLANES, 1), 1)
    head = jnp.where(t_idx < j, rolled_tail, rolled[:, :SUBLANES, :])
    return jnp.concatenate([head, rolled[:, SUBLANES:, :]], axis=1)


def _head_mean(sq_ref, pool):
    halves = [
        jnp.dot(sq_ref[:, k * MXU_DIM:(k + 1) * MXU_DIM], pool, preferred_element_type=jnp.float32)
        for k in range(sq_ref.shape[1] // MXU_DIM)
    ]
    return jnp.concatenate(halves, axis=1)


def _load_weights_bf16(weights, stage, sems):
    stage_rows, stage_cols = stage.shape[1:]
    chunks = []
    for src, dst in weights:
        n_rows, n_cols = src.shape
        for r in range(0, n_rows, stage_rows):
            for c in range(0, n_cols, stage_cols):
                chunks.append((src, dst, r, c, min(stage_cols, n_cols - c)))

    def copy(i):
        src, _, r, c, width = chunks[i]
        slot = i % 2
        return pltpu.make_async_copy(src.at[pl.ds(r, stage_rows), pl.ds(c, width)],
                                     stage.at[slot, :, pl.ds(0, width)], sems.at[slot])

    copy(0).start()
    for i, (_, dst, r, c, width) in enumerate(chunks):
        if i + 1 < len(chunks):
            copy(i + 1).start()
        copy(i).wait()
        dst[r:r + stage_rows, c:c + width] = stage[i % 2, :, :width].astype(dst.dtype)


def _block_kernel(x_ref, mod_ref, n1g_ref, w_in_hbm, lcw_ref, lcb_ref, wg_ref, gab_ref, gxb_ref,
                  ap_ref, scw_ref, log_ref, cog_ref, pool_ref, w_out_hbm, n2g_ref, w1_hbm, w2_hbm,
                  fg_ref, o_ref,
                  lx_tail, cv_tail, h_state, a_slab, b_slab, hb_ref, x1_ref, h2_ref, hid_ref, mlp_ref,
                  xlb_ref, sql_ref, sqc_ref, yb_ref,
                  w_in_ref, w_out_ref, w1_ref, w2_ref, stage_ref, stage_sems):
    step = pl.program_id(0)
    bsz, tc, d = x_ref.shape
    rows = bsz * tc
    f32, bf16 = jnp.float32, jnp.bfloat16

    @pl.when(step == 0)
    def _():
        lx_tail[...] = jnp.zeros_like(lx_tail)
        cv_tail[...] = jnp.zeros_like(cv_tail)
        h_state[...] = jnp.zeros_like(h_state)
        x1_ref[...] = jnp.zeros_like(x1_ref)
        hid_ref[...] = jnp.zeros_like(hid_ref)
        _load_weights_bf16([(w_in_hbm, w_in_ref), (w_out_hbm, w_out_ref), (w1_hbm, w1_ref),
                            (w2_hbm, w2_ref)], stage_ref, stage_sems)

    shift1, scale1, gate1, shift2, scale2, gate2 = [mod_ref[k] for k in range(N_ADA)]

    def mlp_out_piece(k):
        cols = slice(k * MXU_DIM, (k + 1) * MXU_DIM)
        mlp_ref[:, cols] = jnp.dot(hid_ref[...], w2_ref[:, cols], preferred_element_type=f32)

    def proj(k, width):
        u = jnp.dot(hb_ref[...], w_in_ref[:, k:k + width], preferred_element_type=f32)
        return u.reshape(bsz, tc, width)

    mlp_out_piece(0)
    x = x_ref[...]
    h = x * _rms_scale(x) * (n1g_ref[...][None] * (1.0 + scale1)) + shift1
    hb_ref[...] = h.reshape(rows, d).astype(bf16)

    u_lx = proj(0, D_LRU)
    u_c = proj(2 * D_LRU + D_CONV, D_CONV)
    u_v = proj(2 * D_LRU + 2 * D_CONV, D_CONV)
    lcw = lcw_ref[...]
    prev_lx = lx_tail[...]
    xl = u_lx * lcw[3][None, None, :] + lcb_ref[...][None]
    for j in range(1, 4):
        xl = xl + _time_shift(u_lx, prev_lx, j) * lcw[3 - j][None, None, :]
    lx_tail[...] = u_lx[:, tc - SUBLANES:, :]
    xlb_ref[...] = xl.reshape(rows, D_LRU).astype(bf16)
    scw = scw_ref[...]
    cv = u_c * u_v
    prev_cv = cv_tail[...]
    conv = cv * scw[2][None, None, :]
    for j in range(1, 3):
        conv = conv + _time_shift(cv, prev_cv, j) * scw[2 - j][None, None, :]
    cv_tail[...] = cv[:, tc - SUBLANES:, :]

    ap = ap_ref[...]
    softplus = jnp.maximum(ap, 0.0) + jnp.log1p(jnp.exp(-jnp.abs(ap)))
    decay_rate = (-C_GATE * softplus)[None]
    first_row = (lax.broadcasted_iota(jnp.int32, (1, SUBLANES, 1), 1) == 0) & (step == 0)

    def sigmoid(z):
        return 0.5 * jnp.tanh(0.5 * z) + 0.5

    def lru_inputs(k):
        ch = slice(k * MXU_DIM, (k + 1) * MXU_DIM)
        g = jnp.dot(xlb_ref[:, ch], wg_ref[k], preferred_element_type=f32)
        ga = g[:, :MXU_DIM].reshape(bsz, tc, MXU_DIM)
        gx = g[:, MXU_DIM:].reshape(bsz, tc, MXU_DIM)
        r = sigmoid(ga + gab_ref[:, ch][None])
        i = sigmoid(gx + gxb_ref[:, ch][None])
        a = jnp.exp(r * decay_rate[:, :, ch])
        v = 1.0 - a * a
        mult = jnp.where(v > 0.0, v * lax.rsqrt(v), 0.0)
        mult_head = jnp.where(first_row, 1.0, mult[:, :SUBLANES, :])
        mult = jnp.concatenate([mult_head, mult[:, SUBLANES:, :]], axis=1)
        bx = mult * (i * xl[:, :, ch])
        for b in range(bsz):
            for j in range(MXU_DIM // LANES):
                s = k * (MXU_DIM // LANES) + j
                rows_b = pl.ds(b * SCAN_PITCH, tc)
                a_slab[s, rows_b, :] = a[b, :, j * LANES:(j + 1) * LANES]
                b_slab[s, rows_b, :] = bx[b, :, j * LANES:(j + 1) * LANES]

    lru_inputs(0)
    u_b = proj(2 * D_LRU, D_CONV)
    lru_inputs(1)
    u_ly = proj(D_LRU, D_LRU)

    mlp_out_piece(1)
    y_c = (u_b * conv).reshape(rows, D_CONV)
    sqc_ref[...] = (y_c * y_c).astype(bf16)
    gelu_ly = jax.nn.gelu(u_ly)
    pool = pool_ref[...]
    y_c = y_c * lax.rsqrt(_head_mean(sqc_ref, pool) + EPS) * cog_ref[...]
    yb_ref[:, D_LRU:] = y_c.astype(bf16)

    mlp_out_piece(2)

    def scan_step(t, carry):
        new = []
        for s in range(N_SLABS):
            idx = pl.ds(t, bsz, stride=SCAN_PITCH)
            h_t = a_slab[s, idx, :] * carry[s] + b_slab[s, idx, :]
            b_slab[s, idx, :] = h_t
            new.append(h_t)
        return tuple(new)

    h0 = tuple(h_state[:, s * LANES:(s + 1) * LANES] for s in range(N_SLABS))
    h_last = lax.fori_loop(0, tc, scan_step, h0, unroll=True)
    for s in range(N_SLABS):
        h_state[:, s * LANES:(s + 1) * LANES] = h_last[s]
    hl = jnp.stack([
        jnp.concatenate([b_slab[s, pl.ds(b * SCAN_PITCH, tc), :] for s in range(N_SLABS)], axis=1)
        for b in range(bsz)], axis=0)
    y_l = (gelu_ly * hl).reshape(rows, D_LRU)
    sql_ref[...] = (y_l * y_l).astype(bf16)
    y_l = y_l * lax.rsqrt(_head_mean(sql_ref, pool) + EPS) * log_ref[...]
    yb_ref[:, :D_LRU] = y_l.astype(bf16)

    mixed = jnp.dot(yb_ref[...], w_out_ref[...], preferred_element_type=f32)
    mlp_out_piece(3)
    x1_prev = x1_ref[...]
    x1 = x_ref[...] + gate1 * mixed.reshape(bsz, tc, d)
    x1_ref[...] = x1
    h2 = x1 * _rms_scale(x1) * (n2g_ref[...][None] * (1.0 + scale2)) + shift2
    h2_ref[...] = h2.reshape(rows, d).astype(bf16)

    for k in range(D_FF // FF_CHUNK):
        cols = slice(k * FF_CHUNK, (k + 1) * FF_CHUNK)
        hk = jnp.dot(h2_ref[...], w1_ref[:, cols], preferred_element_type=f32)
        hid_ref[:, cols] = jnp.square(jnp.maximum(hk, 0.0)).astype(bf16)
        if k == 0:
            x2 = x1_prev + gate2 * mlp_ref[...].reshape(bsz, tc, d)
            o_ref[...] = x2 * _rms_scale(x2) * fg_ref[...][None]


def _block_diag(w):
    n, i, j = w.shape
    eye = jnp.eye(n, dtype=w.dtype)
    return (eye[:, None, :, None] * w[:, :, None, :]).reshape(n * i, n * j)


def _gate_weights(gate_a_w, gate_x_w):
    tiles = []
    for k in range(N_HEADS // HEADS_PER_MXU_TILE):
        hs = slice(k * HEADS_PER_MXU_TILE, (k + 1) * HEADS_PER_MXU_TILE)
        tiles.append(jnp.concatenate([_block_diag(gate_a_w[hs]), _block_diag(gate_x_w[hs])], axis=1))
    return jnp.stack(tiles).astype(jnp.bfloat16)


def _const_spec(shape):
    if shape is None:
        return pl.BlockSpec(memory_space=pl.ANY)
    return pl.BlockSpec(shape, lambda s: (0,) * len(shape), pipeline_mode=pl.Buffered(1))


@jax.jit
def kernel(x, c, ada_w, ada_b, norm1_g, w_in, lru_conv_w, lru_conv_b, gate_a_w, gate_a_b, gate_x_w,
           gate_x_b, a_param, short_conv_w, lru_out_g, conv_out_g, w_out, norm2_g, w_mlp1, w_mlp2,
           final_g):
    bsz, seq, d = x.shape
    assert (d, seq % TIME_CHUNK, bsz) == (D_MODEL, 0, SUBLANES)
    assert ada_w.shape[0] == 1, "one layer"
    bf16 = jnp.bfloat16
    row = lambda v: v.reshape(1, -1)

    mod = _ada_modulation(c, ada_w[0], ada_b[0])
    mod = mod.reshape(bsz, N_ADA, 1, d).transpose(1, 0, 2, 3)
    pool = _block_diag(jnp.full((HEADS_PER_MXU_TILE, HEAD_DIM, HEAD_DIM), 1.0 / HEAD_DIM, bf16))

    operands = [
        (mod, (N_ADA, bsz, 1, d)),
        (row(norm1_g[0]), (1, d)),
        (w_in.reshape(d, D_IN), None),
        (lru_conv_w[0], (4, D_LRU)),
        (row(lru_conv_b[0]), (1, D_LRU)),
        (_gate_weights(gate_a_w[0], gate_x_w[0]), (D_LRU // MXU_DIM, MXU_DIM, 2 * MXU_DIM)),
        (row(gate_a_b[0]), (1, D_LRU)),
        (row(gate_x_b[0]), (1, D_LRU)),
        (row(a_param[0]), (1, D_LRU)),
        (short_conv_w[0], (3, D_CONV)),
        (row(lru_out_g[0]), (1, D_LRU)),
        (row(conv_out_g[0]), (1, D_CONV)),
        (pool, (MXU_DIM, MXU_DIM)),
        (w_out.reshape(D_LRU + D_CONV, d), None),
        (row(norm2_g[0]), (1, d)),
        (w_mlp1.reshape(d, D_FF), None),
        (w_mlp2.reshape(D_FF, d), None),
        (row(final_g), (1, d)),
    ]
    n_chunks = seq // TIME_CHUNK
    block = (bsz, TIME_CHUNK, d)
    rows = bsz * TIME_CHUNK
    return pl.pallas_call(
        _block_kernel,
        grid=(n_chunks + 1,),
        in_specs=[pl.BlockSpec(block, lambda s: (0, jnp.minimum(s, n_chunks - 1), 0))]
        + [_const_spec(shape) for _, shape in operands],
        out_specs=pl.BlockSpec(block, lambda s: (0, jnp.maximum(s - 1, 0), 0)),
        out_shape=jax.ShapeDtypeStruct(x.shape, x.dtype),
        scratch_shapes=[
            pltpu.VMEM((bsz, SUBLANES, D_LRU), jnp.float32),
            pltpu.VMEM((bsz, SUBLANES, D_CONV), jnp.float32),
            pltpu.VMEM((bsz, D_LRU), jnp.float32),
            pltpu.VMEM((N_SLABS, bsz * SCAN_PITCH, LANES), jnp.float32),
            pltpu.VMEM((N_SLABS, bsz * SCAN_PITCH, LANES), jnp.float32),
            pltpu.VMEM((rows, d), jnp.bfloat16),
            pltpu.VMEM(block, jnp.float32),
            pltpu.VMEM((rows, d), jnp.bfloat16),
            pltpu.VMEM((rows, D_FF), jnp.bfloat16),
            pltpu.VMEM((rows, d), jnp.float32),
            pltpu.VMEM((rows, D_LRU), jnp.bfloat16),
            pltpu.VMEM((rows, D_LRU), jnp.bfloat16),
            pltpu.VMEM((rows, D_CONV), jnp.bfloat16),
            pltpu.VMEM((rows, D_LRU + D_CONV), jnp.bfloat16),
            pltpu.VMEM((d, D_IN), jnp.bfloat16),
            pltpu.VMEM((D_LRU + D_CONV, d), jnp.bfloat16),
            pltpu.VMEM((d, D_FF), jnp.bfloat16),
            pltpu.VMEM((D_FF, d), jnp.bfloat16),
            pltpu.VMEM((2, STAGE_ROWS, STAGE_COLS), jnp.float32),
            pltpu.SemaphoreType.DMA((2,)),
        ],
        compiler_params=pltpu.CompilerParams(
            dimension_semantics=("arbitrary",), vmem_limit_bytes=VMEM_LIMIT_BYTES),
        name="hybrid_block",
    )(x, *[v for v, _ in operands])
```

```python
import jax
import jax.numpy as jnp
from jax import lax
from jax.experimental import pallas as pl
from jax.experimental.pallas import tpu as pltpu

D_MODEL = 1024
D_LRU = 512
D_CONV = 512
N_HEADS = 8
HEAD_DIM = 64
D_FF = 4096
D_IN = 2 * D_LRU + 3 * D_CONV
N_ADA = 6
C_GATE = 8.0
EPS = 1e-6

LANES = 128
SUBLANES = 8
MXU_DIM = 256
HEADS_PER_MXU_TILE = MXU_DIM // HEAD_DIM

TIME_CHUNK = 64
FF_CHUNK = 1024
ADA_BLOCK = 1024
STAGE_ROWS, STAGE_COLS = 256, 512
STAGE_SLOTS = 8
SCAN_PITCH = TIME_CHUNK + SUBLANES
N_SLABS = D_LRU // LANES

VMEM_LIMIT_BYTES = 56 * 1024 * 1024


def _ada_kernel(c_ref, w_ref, b_ref, o_ref):
    sc = jax.nn.silu(c_ref[...]).astype(jnp.bfloat16)
    o_ref[...] = jnp.dot(sc, w_ref[...].astype(jnp.bfloat16),
                         preferred_element_type=jnp.float32) + b_ref[...]


def _ada_modulation(c, ada_w, ada_b):
    bsz, d = c.shape
    n = ada_w.shape[1]
    return pl.pallas_call(
        _ada_kernel,
        grid=(n // ADA_BLOCK,),
        in_specs=[
            pl.BlockSpec((bsz, d), lambda j: (0, 0)),
            pl.BlockSpec((d, ADA_BLOCK), lambda j: (0, j)),
            pl.BlockSpec((1, ADA_BLOCK), lambda j: (0, j)),
        ],
        out_specs=pl.BlockSpec((bsz, ADA_BLOCK), lambda j: (0, j)),
        out_shape=jax.ShapeDtypeStruct((bsz, n), jnp.float32),
        name="ada_modulation",
    )(c, ada_w, ada_b.reshape(1, n))


def _rms_scale(v):
    return lax.rsqrt(jnp.mean(v * v, axis=-1, keepdims=True) + EPS)


def _time_shift(u, prev_tail, j):
    rolled = pltpu.roll(u, j, axis=1)
    rolled_tail = pltpu.roll(prev_tail, j, axis=1)
    t_idx = lax.broadcasted_iota(jnp.int32, (1, SUBLANES, 1), 1)
    head = jnp.where(t_idx < j, rolled_tail, rolled[:, :SUBLANES, :])
    return jnp.concatenate([head, rolled[:, SUBLANES:, :]], axis=1)


def _head_mean(sq_ref, pool):
    halves = [
        jnp.dot(sq_ref[:, k * MXU_DIM:(k + 1) * MXU_DIM], pool, preferred_element_type=jnp.float32)
        for k in range(sq_ref.shape[1] // MXU_DIM)
    ]
    return jnp.concatenate(halves, axis=1)


def _load_weights_bf16(weights, stage, sems):
    n_slots, stage_rows, stage_cols = stage.shape
    chunks = []
    for src, dst in weights:
        n_rows, n_cols = src.shape
        assert n_rows % stage_rows == 0 and n_cols % stage_cols == 0
        for r in range(0, n_rows, stage_rows):
            for c in range(0, n_cols, stage_cols):
                chunks.append((src, dst, r, c))

    def copy(i):
        src, _, r, c = chunks[i]
        slot = i % n_slots
        return pltpu.make_async_copy(src.at[pl.ds(r, stage_rows), pl.ds(c, stage_cols)],
                                     stage.at[slot], sems.at[slot])

    lookahead = n_slots - 1
    for i in range(min(lookahead, len(chunks))):
        copy(i).start()
    for i, (_, dst, r, c) in enumerate(chunks):
        if i + lookahead < len(chunks):
            copy(i + lookahead).start()
        copy(i).wait()
        dst[r:r + stage_rows, c:c + stage_cols] = stage[i % n_slots].astype(dst.dtype)


def _block_kernel(x_ref, mod_ref, n1g_ref, w_in_hbm, lcw_ref, lcb_ref, wg_ref, gab_ref, gxb_ref,
                  ap_ref, scw_ref, log_ref, cog_ref, pool_ref, w_out_hbm, n2g_ref, w1_hbm, w2_hbm,
                  fg_ref, o_ref,
                  lx_tail, cv_tail, h_state, a_slab, b_slab, hb_ref, x1_ref, h2_ref, hid_ref, mlp_ref,
                  xlb_ref, sql_ref, sqc_ref, yb_ref,
                  w_in_ref, w_out_ref, w1_ref, w2_ref, stage_ref, stage_sems):
    step = pl.program_id(0)
    bsz, tc, d = x_ref.shape
    rows = bsz * tc
    f32, bf16 = jnp.float32, jnp.bfloat16

    @pl.when(step == 0)
    def _():
        lx_tail[...] = jnp.zeros_like(lx_tail)
        cv_tail[...] = jnp.zeros_like(cv_tail)
        h_state[...] = jnp.zeros_like(h_state)
        x1_ref[...] = jnp.zeros_like(x1_ref)
        hid_ref[...] = jnp.zeros_like(hid_ref)
        _load_weights_bf16([(w_in_hbm, w_in_ref), (w_out_hbm, w_out_ref), (w1_hbm, w1_ref),
                            (w2_hbm, w2_ref)], stage_ref, stage_sems)

    shift1, scale1, gate1, shift2, scale2, gate2 = [mod_ref[k] for k in range(N_ADA)]

    def mlp_out_piece(k):
        cols = slice(k * MXU_DIM, (k + 1) * MXU_DIM)
        mlp_ref[:, cols] = jnp.dot(hid_ref[...], w2_ref[:, cols], preferred_element_type=f32)

    def proj(k, width):
        u = jnp.dot(hb_ref[...], w_in_ref[:, k:k + width], preferred_element_type=f32)
        return u.reshape(bsz, tc, width)

    mlp_out_piece(0)
    x = x_ref[...]
    h = x * _rms_scale(x) * (n1g_ref[...][None] * (1.0 + scale1)) + shift1
    hb_ref[...] = h.reshape(rows, d).astype(bf16)

    u_lx = proj(0, D_LRU)
    u_c = proj(2 * D_LRU + D_CONV, D_CONV)
    u_v = proj(2 * D_LRU + 2 * D_CONV, D_CONV)
    lcw = lcw_ref[...]
    prev_lx = lx_tail[...]
    xl = u_lx * lcw[3][None, None, :] + lcb_ref[...][None]
    for j in range(1, 4):
        xl = xl + _time_shift(u_lx, prev_lx, j) * lcw[3 - j][None, None, :]
    lx_tail[...] = u_lx[:, tc - SUBLANES:, :]
    xlb_ref[...] = xl.reshape(rows, D_LRU).astype(bf16)
    scw = scw_ref[...]
    cv = u_c * u_v
    prev_cv = cv_tail[...]
    conv = cv * scw[2][None, None, :]
    for j in range(1, 3):
        conv = conv + _time_shift(cv, prev_cv, j) * scw[2 - j][None, None, :]
    cv_tail[...] = cv[:, tc - SUBLANES:, :]

    ap = ap_ref[...]
    softplus = jnp.maximum(ap, 0.0) + jnp.log1p(jnp.exp(-jnp.abs(ap)))
    decay_rate = (-C_GATE * softplus)[None]
    first_row = (lax.broadcasted_iota(jnp.int32, (1, SUBLANES, 1), 1) == 0) & (step == 0)

    def sigmoid(z):
        return 0.5 * jnp.tanh(0.5 * z) + 0.5

    def lru_inputs(k):
        ch = slice(k * MXU_DIM, (k + 1) * MXU_DIM)
        g = jnp.dot(xlb_ref[:, ch], wg_ref[k], preferred_element_type=f32)
        ga = g[:, :MXU_DIM].reshape(bsz, tc, MXU_DIM)
        gx = g[:, MXU_DIM:].reshape(bsz, tc, MXU_DIM)
        r = sigmoid(ga + gab_ref[:, ch][None])
        i = sigmoid(gx + gxb_ref[:, ch][None])
        a = jnp.exp(r * decay_rate[:, :, ch])
        v = 1.0 - a * a
        mult = jnp.where(v > 0.0, v * lax.rsqrt(v), 0.0)
        mult_head = jnp.where(first_row, 1.0, mult[:, :SUBLANES, :])
        mult = jnp.concatenate([mult_head, mult[:, SUBLANES:, :]], axis=1)
        bx = mult * (i * xl[:, :, ch])
        for b in range(bsz):
            for j in range(MXU_DIM // LANES):
                s = k * (MXU_DIM // LANES) + j
                rows_b = pl.ds(b * SCAN_PITCH, tc)
                a_slab[s, rows_b, :] = a[b, :, j * LANES:(j + 1) * LANES]
                b_slab[s, rows_b, :] = bx[b, :, j * LANES:(j + 1) * LANES]

    lru_inputs(0)
    u_b = proj(2 * D_LRU, D_CONV)
    lru_inputs(1)
    u_ly = proj(D_LRU, D_LRU)

    mlp_out_piece(1)
    y_c = (u_b * conv).reshape(rows, D_CONV)
    sqc_ref[...] = (y_c * y_c).astype(bf16)
    gelu_ly = jax.nn.gelu(u_ly)
    pool = pool_ref[...]
    y_c = y_c * lax.rsqrt(_head_mean(sqc_ref, pool) + EPS) * cog_ref[...]
    yb_ref[:, D_LRU:] = y_c.astype(bf16)

    mlp_out_piece(2)

    def scan_step(t, carry):
        new = []
        for s in range(N_SLABS):
            idx = pl.ds(t, bsz, stride=SCAN_PITCH)
            h_t = a_slab[s, idx, :] * carry[s] + b_slab[s, idx, :]
            b_slab[s, idx, :] = h_t
            new.append(h_t)
        return tuple(new)

    h0 = tuple(h_state[:, s * LANES:(s + 1) * LANES] for s in range(N_SLABS))
    h_last = lax.fori_loop(0, tc, scan_step, h0, unroll=True)
    for s in range(N_SLABS):
        h_state[:, s * LANES:(s + 1) * LANES] = h_last[s]
    hl = jnp.stack([
        jnp.concatenate([b_slab[s, pl.ds(b * SCAN_PITCH, tc), :] for s in range(N_SLABS)], axis=1)
        for b in range(bsz)], axis=0)
    y_l = (gelu_ly * hl).reshape(rows, D_LRU)
    sql_ref[...] = (y_l * y_l).astype(bf16)
    y_l = y_l * lax.rsqrt(_head_mean(sql_ref, pool) + EPS) * log_ref[...]
    yb_ref[:, :D_LRU] = y_l.astype(bf16)

    mixed = jnp.dot(yb_ref[...], w_out_ref[...], preferred_element_type=f32)
    mlp_out_piece(3)
    x1_prev = x1_ref[...]
    x1 = x_ref[...] + gate1 * mixed.reshape(bsz, tc, d)
    x1_ref[...] = x1
    h2 = x1 * _rms_scale(x1) * (n2g_ref[...][None] * (1.0 + scale2)) + shift2
    h2_ref[...] = h2.reshape(rows, d).astype(bf16)

    for k in range(D_FF // FF_CHUNK):
        cols = slice(k * FF_CHUNK, (k + 1) * FF_CHUNK)
        hk = jnp.dot(h2_ref[...], w1_ref[:, cols], preferred_element_type=f32)
        hid_ref[:, cols] = jnp.square(jnp.maximum(hk, 0.0)).astype(bf16)
        if k == 0:
            x2 = x1_prev + gate2 * mlp_ref[...].reshape(bsz, tc, d)
            o_ref[...] = x2 * _rms_scale(x2) * fg_ref[...][None]


def _block_diag(w):
    n, i, j = w.shape
    eye = jnp.eye(n, dtype=w.dtype)
    return (eye[:, None, :, None] * w[:, :, None, :]).reshape(n * i, n * j)


def _gate_weights(gate_a_w, gate_x_w):
    tiles = []
    for k in range(N_HEADS // HEADS_PER_MXU_TILE):
        hs = slice(k * HEADS_PER_MXU_TILE, (k + 1) * HEADS_PER_MXU_TILE)
        tiles.append(jnp.concatenate([_block_diag(gate_a_w[hs]), _block_diag(gate_x_w[hs])], axis=1))
    return jnp.stack(tiles).astype(jnp.bfloat16)


def _const_spec(shape):
    if shape is None:
        return pl.BlockSpec(memory_space=pl.ANY)
    return pl.BlockSpec(shape, lambda s: (0,) * len(shape), pipeline_mode=pl.Buffered(1))


@jax.jit
def kernel(x, c, ada_w, ada_b, norm1_g, w_in, lru_conv_w, lru_conv_b, gate_a_w, gate_a_b, gate_x_w,
           gate_x_b, a_param, short_conv_w, lru_out_g, conv_out_g, w_out, norm2_g, w_mlp1, w_mlp2,
           final_g):
    bsz, seq, d = x.shape
    assert (d, seq % TIME_CHUNK, bsz) == (D_MODEL, 0, SUBLANES)
    assert ada_w.shape[0] == 1, "one layer"
    bf16 = jnp.bfloat16
    row = lambda v: v.reshape(1, -1)

    mod = _ada_modulation(c, ada_w[0], ada_b[0])
    mod = mod.reshape(bsz, N_ADA, 1, d).transpose(1, 0, 2, 3)
    pool = _block_diag(jnp.full((HEADS_PER_MXU_TILE, HEAD_DIM, HEAD_DIM), 1.0 / HEAD_DIM, bf16))

    operands = [
        (mod, (N_ADA, bsz, 1, d)),
        (row(norm1_g[0]), (1, d)),
        (w_in.reshape(d, D_IN), None),
        (lru_conv_w[0], (4, D_LRU)),
        (row(lru_conv_b[0]), (1, D_LRU)),
        (_gate_weights(gate_a_w[0], gate_x_w[0]), (D_LRU // MXU_DIM, MXU_DIM, 2 * MXU_DIM)),
        (row(gate_a_b[0]), (1, D_LRU)),
        (row(gate_x_b[0]), (1, D_LRU)),
        (row(a_param[0]), (1, D_LRU)),
        (short_conv_w[0], (3, D_CONV)),
        (row(lru_out_g[0]), (1, D_LRU)),
        (row(conv_out_g[0]), (1, D_CONV)),
        (pool, (MXU_DIM, MXU_DIM)),
        (w_out.reshape(D_LRU + D_CONV, d), None),
        (row(norm2_g[0]), (1, d)),
        (w_mlp1.reshape(d, D_FF), None),
        (w_mlp2.reshape(D_FF, d), None),
        (row(final_g), (1, d)),
    ]
    n_chunks = seq // TIME_CHUNK
    block = (bsz, TIME_CHUNK, d)
    rows = bsz * TIME_CHUNK
    return pl.pallas_call(
        _block_kernel,
        grid=(n_chunks + 1,),
        in_specs=[pl.BlockSpec(block, lambda s: (0, jnp.minimum(s, n_chunks - 1), 0))]
        + [_const_spec(shape) for _, shape in operands],
        out_specs=pl.BlockSpec(block, lambda s: (0, jnp.maximum(s - 1, 0), 0)),
        out_shape=jax.ShapeDtypeStruct(x.shape, x.dtype),
        scratch_shapes=[
            pltpu.VMEM((bsz, SUBLANES, D_LRU), jnp.float32),
            pltpu.VMEM((bsz, SUBLANES, D_CONV), jnp.float32),
            pltpu.VMEM((bsz, D_LRU), jnp.float32),
            pltpu.VMEM((N_SLABS, bsz * SCAN_PITCH, LANES), jnp.float32),
            pltpu.VMEM((N_SLABS, bsz * SCAN_PITCH, LANES), jnp.float32),
            pltpu.VMEM((rows, d), jnp.bfloat16),
            pltpu.VMEM(block, jnp.float32),
            pltpu.VMEM((rows, d), jnp.bfloat16),
            pltpu.VMEM((rows, D_FF), jnp.bfloat16),
            pltpu.VMEM((rows, d), jnp.float32),
            pltpu.VMEM((rows, D_LRU), jnp.bfloat16),
            pltpu.VMEM((rows, D_LRU), jnp.bfloat16),
            pltpu.VMEM((rows, D_CONV), jnp.bfloat16),
            pltpu.VMEM((rows, D_LRU + D_CONV), jnp.bfloat16),
            pltpu.VMEM((d, D_IN), jnp.bfloat16),
            pltpu.VMEM((D_LRU + D_CONV, d), jnp.bfloat16),
            pltpu.VMEM((d, D_FF), jnp.bfloat16),
            pltpu.VMEM((D_FF, d), jnp.bfloat16),
            pltpu.VMEM((STAGE_SLOTS, STAGE_ROWS, STAGE_COLS), jnp.float32),
            pltpu.SemaphoreType.DMA((STAGE_SLOTS,)),
        ],
        compiler_params=pltpu.CompilerParams(
            dimension_semantics=("arbitrary",), vmem_limit_bytes=VMEM_LIMIT_BYTES),
        name="hybrid_block",
    )(x, *[v for v, _ in operands])
```

```python
import jax
import jax.numpy as jnp
from jax import lax
from jax.experimental import pallas as pl
from jax.experimental.pallas import tpu as pltpu

D_MODEL = 1024
D_LRU = 512
D_CONV = 512
N_HEADS = 8
HEAD_DIM = 64
D_FF = 4096
D_IN = 2 * D_LRU + 3 * D_CONV
N_ADA = 6
C_GATE = 8.0
EPS = 1e-6

LANES = 128
SUBLANES = 8
MXU_DIM = 256
HEADS_PER_MXU_TILE = MXU_DIM // HEAD_DIM

TIME_CHUNK = 64
FF_CHUNK = 1024
ADA_BLOCK = 1024
STAGE_ROWS, STAGE_COLS = 256, 512
STAGE_SLOTS = 8
SCAN_PITCH = TIME_CHUNK + SUBLANES
N_SLABS = D_LRU // LANES

VMEM_LIMIT_BYTES = 56 * 1024 * 1024


def _ada_kernel(c_ref, w_ref, b_ref, o_ref):
    sc = jax.nn.silu(c_ref[...]).astype(jnp.bfloat16)
    o_ref[...] = jnp.dot(sc, w_ref[...].astype(jnp.bfloat16),
                         preferred_element_type=jnp.float32) + b_ref[...]


def _ada_modulation(c, ada_w, ada_b):
    bsz, d = c.shape
    n = ada_w.shape[1]
    return pl.pallas_call(
        _ada_kernel,
        grid=(n // ADA_BLOCK,),
        in_specs=[
            pl.BlockSpec((bsz, d), lambda j: (0, 0)),
            pl.BlockSpec((d, ADA_BLOCK), lambda j: (0, j)),
            pl.BlockSpec((1, ADA_BLOCK), lambda j: (0, j)),
        ],
        out_specs=pl.BlockSpec((bsz, ADA_BLOCK), lambda j: (0, j)),
        out_shape=jax.ShapeDtypeStruct((bsz, n), jnp.float32),
        name="ada_modulation",
    )(c, ada_w, ada_b.reshape(1, n))


def _rms_scale(v):
    return lax.rsqrt(jnp.mean(v * v, axis=-1, keepdims=True) + EPS)


def _time_shift(u, prev_tail, j):
    rolled = pltpu.roll(u, j, axis=1)
    rolled_tail = pltpu.roll(prev_tail, j, axis=1)
    t_idx = lax.broadcasted_iota(jnp.int32, (1, SUBLANES, 1), 1)
    head = jnp.where(t_idx < j, rolled_tail, rolled[:, :SUBLANES, :])
    return jnp.concatenate([head, rolled[:, SUBLANES:, :]], axis=1)


def _head_mean(sq_ref, pool):
    halves = [
        jnp.dot(sq_ref[:, k * MXU_DIM:(k + 1) * MXU_DIM], pool, preferred_element_type=jnp.float32)
        for k in range(sq_ref.shape[1] // MXU_DIM)
    ]
    return jnp.concatenate(halves, axis=1)


def _load_weights_bf16(weights, stage, sems):
    n_slots, stage_rows, stage_cols = stage.shape
    chunks = []
    for src, dst in weights:
        n_rows, n_cols = src.shape
        assert n_rows % stage_rows == 0 and n_cols % stage_cols == 0
        for r in range(0, n_rows, stage_rows):
            for c in range(0, n_cols, stage_cols):
                chunks.append((src, dst, r, c))

    def copy(i):
        src, _, r, c = chunks[i]
        slot = i % n_slots
        return pltpu.make_async_copy(src.at[pl.ds(r, stage_rows), pl.ds(c, stage_cols)],
                                     stage.at[slot], sems.at[slot])

    lookahead = n_slots - 1
    for i in range(min(lookahead, len(chunks))):
        copy(i).start()
    for i, (_, dst, r, c) in enumerate(chunks):
        if i + lookahead < len(chunks):
            copy(i + lookahead).start()
        copy(i).wait()
        dst[r:r + stage_rows, c:c + stage_cols] = stage[i % n_slots].astype(dst.dtype)


def _block_kernel(x_ref, mod_ref, n1g_ref, w_in_hbm, lcw_ref, lcb_ref, wg_ref, gab_ref, gxb_ref,
                  ap_ref, scw_ref, log_ref, cog_ref, pool_ref, w_out_hbm, n2g_ref, w1_hbm, w2_hbm,
                  fg_ref, o_ref,
                  lx_tail, cv_tail, h_state, a_slab, b_slab, hb_ref, x1_ref, h2_ref, hid_ref, mlp_ref,
                  xlb_ref, sql_ref, sqc_ref, yb_ref,
                  w_in_ref, w_out_ref, w1_ref, w2_ref, stage_ref, stage_sems):
    step = pl.program_id(0)
    last_step = pl.num_programs(0) - 1
    bsz, tc, d = x_ref.shape
    rows = bsz * tc
    f32, bf16 = jnp.float32, jnp.bfloat16

    def run(mix, out, first):
        shift1, scale1, gate1, shift2, scale2, gate2 = [mod_ref[k] for k in range(N_ADA)]

        def mlp_out_piece(k):
            cols = slice(k * MXU_DIM, (k + 1) * MXU_DIM)
            mlp_ref[:, cols] = jnp.dot(hid_ref[...], w2_ref[:, cols], preferred_element_type=f32)

        def proj(k, width):
            u = jnp.dot(hb_ref[...], w_in_ref[:, k:k + width], preferred_element_type=f32)
            return u.reshape(bsz, tc, width)

        if out:
            mlp_out_piece(0)
        if not mix:
            for k in range(1, d // MXU_DIM):
                mlp_out_piece(k)
            x2 = x1_ref[...] + gate2 * mlp_ref[...].reshape(bsz, tc, d)
            o_ref[...] = x2 * _rms_scale(x2) * fg_ref[...][None]
            return
        x = x_ref[...]
        h = x * _rms_scale(x) * (n1g_ref[...][None] * (1.0 + scale1)) + shift1
        hb_ref[...] = h.reshape(rows, d).astype(bf16)

        u_lx = proj(0, D_LRU)
        u_c = proj(2 * D_LRU + D_CONV, D_CONV)
        u_v = proj(2 * D_LRU + 2 * D_CONV, D_CONV)
        lcw = lcw_ref[...]
        prev_lx = lx_tail[...]
        xl = u_lx * lcw[3][None, None, :] + lcb_ref[...][None]
        for j in range(1, 4):
            xl = xl + _time_shift(u_lx, prev_lx, j) * lcw[3 - j][None, None, :]
        lx_tail[...] = u_lx[:, tc - SUBLANES:, :]
        xlb_ref[...] = xl.reshape(rows, D_LRU).astype(bf16)
        scw = scw_ref[...]
        cv = u_c * u_v
        prev_cv = cv_tail[...]
        conv = cv * scw[2][None, None, :]
        for j in range(1, 3):
            conv = conv + _time_shift(cv, prev_cv, j) * scw[2 - j][None, None, :]
        cv_tail[...] = cv[:, tc - SUBLANES:, :]

        ap = ap_ref[...]
        softplus = jnp.maximum(ap, 0.0) + jnp.log1p(jnp.exp(-jnp.abs(ap)))
        decay_rate = (-C_GATE * softplus)[None]
        first_row = lax.broadcasted_iota(jnp.int32, (1, SUBLANES, 1), 1) == 0

        def sigmoid(z):
            return 0.5 * jnp.tanh(0.5 * z) + 0.5

        def lru_inputs(k):
            ch = slice(k * MXU_DIM, (k + 1) * MXU_DIM)
            g = jnp.dot(xlb_ref[:, ch], wg_ref[k], preferred_element_type=f32)
            ga = g[:, :MXU_DIM].reshape(bsz, tc, MXU_DIM)
            gx = g[:, MXU_DIM:].reshape(bsz, tc, MXU_DIM)
            r = sigmoid(ga + gab_ref[:, ch][None])
            i = sigmoid(gx + gxb_ref[:, ch][None])
            a = jnp.exp(r * decay_rate[:, :, ch])
            v = 1.0 - a * a
            mult = jnp.where(v > 0.0, v * lax.rsqrt(v), 0.0)
            if first:
                mult_head = jnp.where(first_row, 1.0, mult[:, :SUBLANES, :])
                mult = jnp.concatenate([mult_head, mult[:, SUBLANES:, :]], axis=1)
            bx = mult * (i * xl[:, :, ch])
            for b in range(bsz):
                for j in range(MXU_DIM // LANES):
                    s = k * (MXU_DIM // LANES) + j
                    rows_b = pl.ds(b * SCAN_PITCH, tc)
                    a_slab[s, rows_b, :] = a[b, :, j * LANES:(j + 1) * LANES]
                    b_slab[s, rows_b, :] = bx[b, :, j * LANES:(j + 1) * LANES]

        lru_inputs(0)
        u_b = proj(2 * D_LRU, D_CONV)
        lru_inputs(1)
        u_ly = proj(D_LRU, D_LRU)

        if out:
            mlp_out_piece(1)
        y_c = (u_b * conv).reshape(rows, D_CONV)
        sqc_ref[...] = (y_c * y_c).astype(bf16)
        gelu_ly = jax.nn.gelu(u_ly)
        pool = pool_ref[...]
        y_c = y_c * lax.rsqrt(_head_mean(sqc_ref, pool) + EPS) * cog_ref[...]
        yb_ref[:, D_LRU:] = y_c.astype(bf16)

        if out:
            mlp_out_piece(2)

        def scan_step(t, carry):
            new = []
            for s in range(N_SLABS):
                idx = pl.ds(t, bsz, stride=SCAN_PITCH)
                h_t = a_slab[s, idx, :] * carry[s] + b_slab[s, idx, :]
                b_slab[s, idx, :] = h_t
                new.append(h_t)
            return tuple(new)

        h0 = tuple(h_state[:, s * LANES:(s + 1) * LANES] for s in range(N_SLABS))
        h_last = lax.fori_loop(0, tc, scan_step, h0, unroll=True)
        for s in range(N_SLABS):
            h_state[:, s * LANES:(s + 1) * LANES] = h_last[s]
        hl = jnp.stack([
            jnp.concatenate([b_slab[s, pl.ds(b * SCAN_PITCH, tc), :] for s in range(N_SLABS)], axis=1)
            for b in range(bsz)], axis=0)
        y_l = (gelu_ly * hl).reshape(rows, D_LRU)
        sql_ref[...] = (y_l * y_l).astype(bf16)
        y_l = y_l * lax.rsqrt(_head_mean(sql_ref, pool) + EPS) * log_ref[...]
        yb_ref[:, :D_LRU] = y_l.astype(bf16)

        mixed = jnp.dot(yb_ref[...], w_out_ref[...], preferred_element_type=f32)
        if out:
            mlp_out_piece(3)
            x1_prev = x1_ref[...]
        x1 = x_ref[...] + gate1 * mixed.reshape(bsz, tc, d)
        x1_ref[...] = x1
        h2 = x1 * _rms_scale(x1) * (n2g_ref[...][None] * (1.0 + scale2)) + shift2
        h2_ref[...] = h2.reshape(rows, d).astype(bf16)

        for k in range(D_FF // FF_CHUNK):
            cols = slice(k * FF_CHUNK, (k + 1) * FF_CHUNK)
            hk = jnp.dot(h2_ref[...], w1_ref[:, cols], preferred_element_type=f32)
            hid_ref[:, cols] = jnp.square(jnp.maximum(hk, 0.0)).astype(bf16)
            if k == 0 and out:
                x2 = x1_prev + gate2 * mlp_ref[...].reshape(bsz, tc, d)
                o_ref[...] = x2 * _rms_scale(x2) * fg_ref[...][None]

    @pl.when(step == 0)
    def _():
        lx_tail[...] = jnp.zeros_like(lx_tail)
        cv_tail[...] = jnp.zeros_like(cv_tail)
        h_state[...] = jnp.zeros_like(h_state)
        _load_weights_bf16([(w_in_hbm, w_in_ref), (w_out_hbm, w_out_ref), (w1_hbm, w1_ref),
                            (w2_hbm, w2_ref)], stage_ref, stage_sems)
        run(mix=True, out=False, first=True)

    @pl.when((step > 0) & (step < last_step))
    def _():
        run(mix=True, out=True, first=False)

    @pl.when(step == last_step)
    def _():
        run(mix=False, out=True, first=False)


def _block_diag(w):
    n, i, j = w.shape
    eye = jnp.eye(n, dtype=w.dtype)
    return (eye[:, None, :, None] * w[:, :, None, :]).reshape(n * i, n * j)


def _gate_weights(gate_a_w, gate_x_w):
    tiles = []
    for k in range(N_HEADS // HEADS_PER_MXU_TILE):
        hs = slice(k * HEADS_PER_MXU_TILE, (k + 1) * HEADS_PER_MXU_TILE)
        tiles.append(jnp.concatenate([_block_diag(gate_a_w[hs]), _block_diag(gate_x_w[hs])], axis=1))
    return jnp.stack(tiles).astype(jnp.bfloat16)


def _const_spec(shape):
    if shape is None:
        return pl.BlockSpec(memory_space=pl.ANY)
    return pl.BlockSpec(shape, lambda s: (0,) * len(shape), pipeline_mode=pl.Buffered(1))


@jax.jit
def kernel(x, c, ada_w, ada_b, norm1_g, w_in, lru_conv_w, lru_conv_b, gate_a_w, gate_a_b, gate_x_w,
           gate_x_b, a_param, short_conv_w, lru_out_g, conv_out_g, w_out, norm2_g, w_mlp1, w_mlp2,
           final_g):
    bsz, seq, d = x.shape
    assert (d, seq % TIME_CHUNK, bsz) == (D_MODEL, 0, SUBLANES)
    assert ada_w.shape[0] == 1, "one layer"
    bf16 = jnp.bfloat16
    row = lambda v: v.reshape(1, -1)

    mod = _ada_modulation(c, ada_w[0], ada_b[0])
    mod = mod.reshape(bsz, N_ADA, 1, d).transpose(1, 0, 2, 3)
    pool = _block_diag(jnp.full((HEADS_PER_MXU_TILE, HEAD_DIM, HEAD_DIM), 1.0 / HEAD_DIM, bf16))

    operands = [
        (mod, (N_ADA, bsz, 1, d)),
        (row(norm1_g[0]), (1, d)),
        (w_in.reshape(d, D_IN), None),
        (lru_conv_w[0], (4, D_LRU)),
        (row(lru_conv_b[0]), (1, D_LRU)),
        (_gate_weights(gate_a_w[0], gate_x_w[0]), (D_LRU // MXU_DIM, MXU_DIM, 2 * MXU_DIM)),
        (row(gate_a_b[0]), (1, D_LRU)),
        (row(gate_x_b[0]), (1, D_LRU)),
        (row(a_param[0]), (1, D_LRU)),
        (short_conv_w[0], (3, D_CONV)),
        (row(lru_out_g[0]), (1, D_LRU)),
        (row(conv_out_g[0]), (1, D_CONV)),
        (pool, (MXU_DIM, MXU_DIM)),
        (w_out.reshape(D_LRU + D_CONV, d), None),
        (row(norm2_g[0]), (1, d)),
        (w_mlp1.reshape(d, D_FF), None),
        (w_mlp2.reshape(D_FF, d), None),
        (row(final_g), (1, d)),
    ]
    n_chunks = seq // TIME_CHUNK
    block = (bsz, TIME_CHUNK, d)
    rows = bsz * TIME_CHUNK
    return pl.pallas_call(
        _block_kernel,
        grid=(n_chunks + 1,),
        in_specs=[pl.BlockSpec(block, lambda s: (0, jnp.minimum(s, n_chunks - 1), 0))]
        + [_const_spec(shape) for _, shape in operands],
        out_specs=pl.BlockSpec(block, lambda s: (0, jnp.maximum(s - 1, 0), 0)),
        out_shape=jax.ShapeDtypeStruct(x.shape, x.dtype),
        scratch_shapes=[
            pltpu.VMEM((bsz, SUBLANES, D_LRU), jnp.float32),
            pltpu.VMEM((bsz, SUBLANES, D_CONV), jnp.float32),
            pltpu.VMEM((bsz, D_LRU), jnp.float32),
            pltpu.VMEM((N_SLABS, bsz * SCAN_PITCH, LANES), jnp.float32),
            pltpu.VMEM((N_SLABS, bsz * SCAN_PITCH, LANES), jnp.float32),
            pltpu.VMEM((rows, d), jnp.bfloat16),
            pltpu.VMEM(block, jnp.float32),
            pltpu.VMEM((rows, d), jnp.bfloat16),
            pltpu.VMEM((rows, D_FF), jnp.bfloat16),
            pltpu.VMEM((rows, d), jnp.float32),
            pltpu.VMEM((rows, D_LRU), jnp.bfloat16),
            pltpu.VMEM((rows, D_LRU), jnp.bfloat16),
            pltpu.VMEM((rows, D_CONV), jnp.bfloat16),
            pltpu.VMEM((rows, D_LRU + D_CONV), jnp.bfloat16),
            pltpu.VMEM((d, D_IN), jnp.bfloat16),
            pltpu.VMEM((D_LRU + D_CONV, d), jnp.bfloat16),
            pltpu.VMEM((d, D_FF), jnp.bfloat16),
            pltpu.VMEM((D_FF, d), jnp.bfloat16),
            pltpu.VMEM((STAGE_SLOTS, STAGE_ROWS, STAGE_COLS), jnp.float32),
            pltpu.SemaphoreType.DMA((STAGE_SLOTS,)),
        ],
        compiler_params=pltpu.CompilerParams(
            dimension_semantics=("arbitrary",), vmem_limit_bytes=VMEM_LIMIT_BYTES),
        name="hybrid_block",
    )(x, *[v for v, _ in operands])
```

```python
import jax
import jax.numpy as jnp
from jax import lax
from jax.experimental import pallas as pl
from jax.experimental.pallas import tpu as pltpu

D_MODEL = 1024
D_LRU = 512
D_CONV = 512
N_HEADS = 8
HEAD_DIM = 64
D_FF = 4096
D_IN = 2 * D_LRU + 3 * D_CONV
N_ADA = 6
C_GATE = 8.0
EPS = 1e-6
LOG2_E = 1.4426950408889634
GELU_C0 = 0.7978845608028654
GELU_C1 = 0.044715

LANES = 128
SUBLANES = 8
MXU_DIM = 256
HEADS_PER_MXU_TILE = MXU_DIM // HEAD_DIM

TIME_CHUNK = 64
FF_CHUNK = 1024
ADA_BLOCK = 1024
STAGE_ROWS, STAGE_COLS = 256, 512
STAGE_SLOTS = 8
SCAN_PITCH = TIME_CHUNK + SUBLANES
N_SLABS = D_LRU // LANES

VMEM_LIMIT_BYTES = 56 * 1024 * 1024


def _ada_kernel(c_ref, w_ref, b_ref, o_ref):
    sc = jax.nn.silu(c_ref[...]).astype(jnp.bfloat16)
    o_ref[...] = jnp.dot(sc, w_ref[...].astype(jnp.bfloat16),
                         preferred_element_type=jnp.float32) + b_ref[...]


def _ada_modulation(c, ada_w, ada_b):
    bsz, d = c.shape
    n = ada_w.shape[1]
    return pl.pallas_call(
        _ada_kernel,
        grid=(n // ADA_BLOCK,),
        in_specs=[
            pl.BlockSpec((bsz, d), lambda j: (0, 0)),
            pl.BlockSpec((d, ADA_BLOCK), lambda j: (0, j)),
            pl.BlockSpec((1, ADA_BLOCK), lambda j: (0, j)),
        ],
        out_specs=pl.BlockSpec((bsz, ADA_BLOCK), lambda j: (0, j)),
        out_shape=jax.ShapeDtypeStruct((bsz, n), jnp.float32),
        name="ada_modulation",
    )(c, ada_w, ada_b.reshape(1, n))


def _rms_scale(v):
    return lax.rsqrt(jnp.mean(v * v, axis=-1, keepdims=True) + EPS)


def _time_shift(u, prev_tail, j):
    rolled = pltpu.roll(u, j, axis=1)
    rolled_tail = pltpu.roll(prev_tail, j, axis=1)
    t_idx = lax.broadcasted_iota(jnp.int32, (1, SUBLANES, 1), 1)
    head = jnp.where(t_idx < j, rolled_tail, rolled[:, :SUBLANES, :])
    return jnp.concatenate([head, rolled[:, SUBLANES:, :]], axis=1)


def _head_mean(sq_ref, pool):
    halves = [
        jnp.dot(sq_ref[:, k * MXU_DIM:(k + 1) * MXU_DIM], pool, preferred_element_type=jnp.float32)
        for k in range(sq_ref.shape[1] // MXU_DIM)
    ]
    return jnp.concatenate(halves, axis=1)


def _load_weights_bf16(weights, stage, sems):
    n_slots, stage_rows, stage_cols = stage.shape
    chunks = []
    for src, dst in weights:
        n_rows, n_cols = src.shape
        assert n_rows % stage_rows == 0 and n_cols % stage_cols == 0
        for r in range(0, n_rows, stage_rows):
            for c in range(0, n_cols, stage_cols):
                chunks.append((src, dst, r, c))

    def copy(i):
        src, _, r, c = chunks[i]
        slot = i % n_slots
        return pltpu.make_async_copy(src.at[pl.ds(r, stage_rows), pl.ds(c, stage_cols)],
                                     stage.at[slot], sems.at[slot])

    lookahead = n_slots - 1
    for i in range(min(lookahead, len(chunks))):
        copy(i).start()
    for i, (_, dst, r, c) in enumerate(chunks):
        if i + lookahead < len(chunks):
            copy(i + lookahead).start()
        copy(i).wait()
        dst[r:r + stage_rows, c:c + stage_cols] = stage[i % n_slots].astype(dst.dtype)


def _block_kernel(x_ref, mod_ref, n1g_ref, w_in_hbm, lcw_ref, lcb_ref, wg_ref, gab_ref, gxb_ref,
                  ap_ref, scw_ref, log_ref, cog_ref, pool_ref, w_out_hbm, n2g_ref, w1_hbm, w2_hbm,
                  fg_ref, o_ref,
                  lx_tail, cv_tail, h_state, a_slab, b_slab, h_slab, hb_ref, x1_ref, h2_ref, hid_ref, mlp_ref,
                  xlb_ref, sql_ref, sqc_ref, yb_ref,
                  w_in_ref, w_out_ref, w1_ref, w2_ref, stage_ref, stage_sems):
    step = pl.program_id(0)
    bsz, tc, d = x_ref.shape
    rows = bsz * tc
    f32, bf16 = jnp.float32, jnp.bfloat16

    def run(mix, out, first):
        shift1, scale1, gate1, shift2, scale2, gate2 = [mod_ref[k] for k in range(N_ADA)]

        def mlp_out_piece(k):
            cols = slice(k * MXU_DIM, (k + 1) * MXU_DIM)
            mlp_ref[:, cols] = jnp.dot(hid_ref[...], w2_ref[:, cols], preferred_element_type=f32)

        def proj(k, width):
            u = jnp.dot(hb_ref[...], w_in_ref[:, k:k + width], preferred_element_type=f32)
            return u.reshape(bsz, tc, width)

        if out:
            mlp_out_piece(0)
        if not mix:
            for k in range(1, d // MXU_DIM):
                mlp_out_piece(k)
            x2 = x1_ref[...] + gate2 * mlp_ref[...].reshape(bsz, tc, d)
            o_ref[...] = x2 * _rms_scale(x2) * fg_ref[...][None]
            return
        x = x_ref[...]
        h = x * _rms_scale(x) * (n1g_ref[...][None] * (1.0 + scale1)) + shift1
        hb_ref[...] = h.reshape(rows, d).astype(bf16)

        u_lx = proj(0, D_LRU)
        u_c = proj(2 * D_LRU + D_CONV, D_CONV)
        u_v = proj(2 * D_LRU + 2 * D_CONV, D_CONV)
        lcw = lcw_ref[...]
        prev_lx = lx_tail[...]
        xl = u_lx * lcw[3][None, None, :] + lcb_ref[...][None]
        for j in range(1, 4):
            xl = xl + _time_shift(u_lx, prev_lx, j) * lcw[3 - j][None, None, :]
        lx_tail[...] = u_lx[:, tc - SUBLANES:, :]
        xlb_ref[...] = xl.reshape(rows, D_LRU).astype(bf16)
        scw = scw_ref[...]
        cv = u_c * u_v
        prev_cv = cv_tail[...]
        conv = cv * scw[2][None, None, :]
        for j in range(1, 3):
            conv = conv + _time_shift(cv, prev_cv, j) * scw[2 - j][None, None, :]
        cv_tail[...] = cv[:, tc - SUBLANES:, :]

        ap = ap_ref[...]
        softplus = jnp.maximum(ap, 0.0) + jnp.log1p(jnp.exp(-jnp.abs(ap)))
        half_rate = ((-0.5 * C_GATE * LOG2_E) * softplus)[None]
        half_ab = (0.5 * gab_ref[...])[None]
        half_xb = (0.5 * gxb_ref[...])[None]
        first_row = (lax.broadcasted_iota(jnp.int32, (1, SUBLANES, 1), 1) == 0) & (step == 0)

        def lru_inputs(k):
            ch = slice(k * MXU_DIM, (k + 1) * MXU_DIM)
            g = jnp.dot(xlb_ref[:, ch], wg_ref[k], preferred_element_type=f32)
            ga = g[:, :MXU_DIM].reshape(bsz, tc, MXU_DIM)
            gx = g[:, MXU_DIM:].reshape(bsz, tc, MXU_DIM)
            a = jnp.exp2(jnp.tanh(ga + half_ab[:, :, ch]) * half_rate[:, :, ch] + half_rate[:, :, ch])
            i = 0.5 * jnp.tanh(gx + half_xb[:, :, ch]) + 0.5
            v = 1.0 - a * a
            mult = jnp.where(v > 0.0, v * lax.rsqrt(v), 0.0)
            if first:
                mult_head = jnp.where(first_row, 1.0, mult[:, :SUBLANES, :])
                mult = jnp.concatenate([mult_head, mult[:, SUBLANES:, :]], axis=1)
            bx = mult * (i * xl[:, :, ch])
            for b in range(bsz):
                for j in range(MXU_DIM // LANES):
                    s = k * (MXU_DIM // LANES) + j
                    rows_b = pl.ds(b * SCAN_PITCH, tc)
                    a_slab[s, rows_b, :] = a[b, :, j * LANES:(j + 1) * LANES]
                    b_slab[s, rows_b, :] = bx[b, :, j * LANES:(j + 1) * LANES]

        lru_inputs(0)
        u_b = proj(2 * D_LRU, D_CONV)
        lru_inputs(1)
        u_ly = proj(D_LRU, D_LRU)

        if out:
            mlp_out_piece(1)
        y_c = (u_b * conv).reshape(rows, D_CONV)
        sqc_ref[...] = (y_c * y_c).astype(bf16)
        half_ly = 0.5 * u_ly
        gelu_ly = half_ly * jnp.tanh(u_ly * ((GELU_C0 * GELU_C1) * (u_ly * u_ly) + GELU_C0)) + half_ly
        pool = pool_ref[...]
        y_c = y_c * lax.rsqrt(_head_mean(sqc_ref, pool) + EPS) * cog_ref[...]
        yb_ref[:, D_LRU:] = y_c.astype(bf16)

        if out:
            mlp_out_piece(2)

        def scan_step(t, carry):
            new = []
            for s in range(N_SLABS):
                idx = pl.ds(t, bsz, stride=SCAN_PITCH)
                h_t = a_slab[s, idx, :] * carry[s] + b_slab[s, idx, :]
                h_slab[s, idx, :] = h_t
                new.append(h_t)
            return tuple(new)

        h0 = tuple(h_state[:, s * LANES:(s + 1) * LANES] for s in range(N_SLABS))
        h_last = lax.fori_loop(0, tc, scan_step, h0, unroll=True)
        for s in range(N_SLABS):
            h_state[:, s * LANES:(s + 1) * LANES] = h_last[s]
        hl = jnp.stack([
            jnp.concatenate([h_slab[s, pl.ds(b * SCAN_PITCH, tc), :] for s in range(N_SLABS)], axis=1)
            for b in range(bsz)], axis=0)
        y_l = (gelu_ly * hl).reshape(rows, D_LRU)
        sql_ref[...] = (y_l * y_l).astype(bf16)
        y_l = y_l * lax.rsqrt(_head_mean(sql_ref, pool) + EPS) * log_ref[...]
        yb_ref[:, :D_LRU] = y_l.astype(bf16)

        mixed = jnp.dot(yb_ref[...], w_out_ref[...], preferred_element_type=f32)
        if out:
            mlp_out_piece(3)
            x1_prev = x1_ref[...]
        x1 = x_ref[...] + gate1 * mixed.reshape(bsz, tc, d)
        x1_ref[...] = x1
        h2 = x1 * _rms_scale(x1) * (n2g_ref[...][None] * (1.0 + scale2)) + shift2
        h2_ref[...] = h2.reshape(rows, d).astype(bf16)

        for k in range(D_FF // FF_CHUNK):
            cols = slice(k * FF_CHUNK, (k + 1) * FF_CHUNK)
            hk = jnp.dot(h2_ref[...], w1_ref[:, cols], preferred_element_type=f32)
            hid_ref[:, cols] = jnp.square(jnp.maximum(hk, 0.0)).astype(bf16)
            if k == 0 and out:
                x2 = x1_prev + gate2 * mlp_ref[...].reshape(bsz, tc, d)
                o_ref[...] = x2 * _rms_scale(x2) * fg_ref[...][None]

    @pl.when(step == 0)
    def _():
        lx_tail[...] = jnp.zeros_like(lx_tail)
        cv_tail[...] = jnp.zeros_like(cv_tail)
        h_state[...] = jnp.zeros_like(h_state)
        x1_ref[...] = jnp.zeros_like(x1_ref)
        hid_ref[...] = jnp.zeros_like(hid_ref)
        _load_weights_bf16([(w_in_hbm, w_in_ref), (w_out_hbm, w_out_ref), (w1_hbm, w1_ref),
                            (w2_hbm, w2_ref)], stage_ref, stage_sems)

    run(mix=True, out=True, first=True)


def _block_diag(w):
    n, i, j = w.shape
    eye = jnp.eye(n, dtype=w.dtype)
    return (eye[:, None, :, None] * w[:, :, None, :]).reshape(n * i, n * j)


def _gate_weights(gate_a_w, gate_x_w):
    tiles = []
    for k in range(N_HEADS // HEADS_PER_MXU_TILE):
        hs = slice(k * HEADS_PER_MXU_TILE, (k + 1) * HEADS_PER_MXU_TILE)
        tiles.append(jnp.concatenate([_block_diag(gate_a_w[hs]), _block_diag(gate_x_w[hs])], axis=1))
    return (0.5 * jnp.stack(tiles)).astype(jnp.bfloat16)


def _const_spec(shape):
    if shape is None:
        return pl.BlockSpec(memory_space=pl.ANY)
    return pl.BlockSpec(shape, lambda s: (0,) * len(shape), pipeline_mode=pl.Buffered(1))


@jax.jit
def kernel(x, c, ada_w, ada_b, norm1_g, w_in, lru_conv_w, lru_conv_b, gate_a_w, gate_a_b, gate_x_w,
           gate_x_b, a_param, short_conv_w, lru_out_g, conv_out_g, w_out, norm2_g, w_mlp1, w_mlp2,
           final_g):
    bsz, seq, d = x.shape
    assert (d, seq % TIME_CHUNK, bsz) == (D_MODEL, 0, SUBLANES)
    assert ada_w.shape[0] == 1, "one layer"
    bf16 = jnp.bfloat16
    row = lambda v: v.reshape(1, -1)

    mod = _ada_modulation(c, ada_w[0], ada_b[0])
    mod = mod.reshape(bsz, N_ADA, 1, d).transpose(1, 0, 2, 3)
    pool = _block_diag(jnp.full((HEADS_PER_MXU_TILE, HEAD_DIM, HEAD_DIM), 1.0 / HEAD_DIM, bf16))

    operands = [
        (mod, (N_ADA, bsz, 1, d)),
        (row(norm1_g[0]), (1, d)),
        (w_in.reshape(d, D_IN), None),
        (lru_conv_w[0], (4, D_LRU)),
        (row(lru_conv_b[0]), (1, D_LRU)),
        (_gate_weights(gate_a_w[0], gate_x_w[0]), (D_LRU // MXU_DIM, MXU_DIM, 2 * MXU_DIM)),
        (row(gate_a_b[0]), (1, D_LRU)),
        (row(gate_x_b[0]), (1, D_LRU)),
        (row(a_param[0]), (1, D_LRU)),
        (short_conv_w[0], (3, D_CONV)),
        (row(lru_out_g[0]), (1, D_LRU)),
        (row(conv_out_g[0]), (1, D_CONV)),
        (pool, (MXU_DIM, MXU_DIM)),
        (w_out.reshape(D_LRU + D_CONV, d), None),
        (row(norm2_g[0]), (1, d)),
        (w_mlp1.reshape(d, D_FF), None),
        (w_mlp2.reshape(D_FF, d), None),
        (row(final_g), (1, d)),
    ]
    n_chunks = seq // TIME_CHUNK
    block = (bsz, TIME_CHUNK, d)
    rows = bsz * TIME_CHUNK
    return pl.pallas_call(
        _block_kernel,
        grid=(n_chunks + 1,),
        in_specs=[pl.BlockSpec(block, lambda s: (0, jnp.minimum(s, n_chunks - 1), 0))]
        + [_const_spec(shape) for _, shape in operands],
        out_specs=pl.BlockSpec(block, lambda s: (0, jnp.maximum(s - 1, 0), 0)),
        out_shape=jax.ShapeDtypeStruct(x.shape, x.dtype),
        scratch_shapes=[
            pltpu.VMEM((bsz, SUBLANES, D_LRU), jnp.float32),
            pltpu.VMEM((bsz, SUBLANES, D_CONV), jnp.float32),
            pltpu.VMEM((bsz, D_LRU), jnp.float32),
            pltpu.VMEM((N_SLABS, bsz * SCAN_PITCH, LANES), jnp.float32),
            pltpu.VMEM((N_SLABS, bsz * SCAN_PITCH, LANES), jnp.float32),
            pltpu.VMEM((N_SLABS, bsz * SCAN_PITCH, LANES), jnp.float32),
            pltpu.VMEM((rows, d), jnp.bfloat16),
            pltpu.VMEM(block, jnp.float32),
            pltpu.VMEM((rows, d), jnp.bfloat16),
            pltpu.VMEM((rows, D_FF), jnp.bfloat16),
            pltpu.VMEM((rows, d), jnp.float32),
            pltpu.VMEM((rows, D_LRU), jnp.bfloat16),
            pltpu.VMEM((rows, D_LRU), jnp.bfloat16),
            pltpu.VMEM((rows, D_CONV), jnp.bfloat16),
            pltpu.VMEM((rows, D_LRU + D_CONV), jnp.bfloat16),
            pltpu.VMEM((d, D_IN), jnp.bfloat16),
            pltpu.VMEM((D_LRU + D_CONV, d), jnp.bfloat16),
            pltpu.VMEM((d, D_FF), jnp.bfloat16),
            pltpu.VMEM((D_FF, d), jnp.bfloat16),
            pltpu.VMEM((STAGE_SLOTS, STAGE_ROWS, STAGE_COLS), jnp.float32),
            pltpu.SemaphoreType.DMA((STAGE_SLOTS,)),
        ],
        compiler_params=pltpu.CompilerParams(
            dimension_semantics=("arbitrary",), vmem_limit_bytes=VMEM_LIMIT_BYTES),
        name="hybrid_block",
    )(x, *[v for v, _ in operands])
```

```python
import jax
import jax.numpy as jnp
from jax import lax
from jax.experimental import pallas as pl
from jax.experimental.pallas import tpu as pltpu

D_MODEL = 1024
D_LRU = 512
D_CONV = 512
N_HEADS = 8
HEAD_DIM = 64
D_FF = 4096
D_IN = 2 * D_LRU + 3 * D_CONV
N_ADA = 6
C_GATE = 8.0
EPS = 1e-6
LOG2_E = 1.4426950408889634
GELU_C0 = 0.7978845608028654
GELU_C1 = 0.044715

LANES = 128
SUBLANES = 8
MXU_DIM = 256
HEADS_PER_MXU_TILE = MXU_DIM // HEAD_DIM

TIME_CHUNK = 64
FF_CHUNK = 1024
STAGE_ROWS, STAGE_COLS = 256, 512
STAGE_SLOTS = 8
SCAN_PITCH = TIME_CHUNK + SUBLANES
N_SLABS = D_LRU // LANES

VMEM_LIMIT_BYTES = 56 * 1024 * 1024


def _rms_scale(v):
    return lax.rsqrt(jnp.mean(v * v, axis=-1, keepdims=True) + EPS)


def _time_shift(u, prev_tail, j):
    rolled = pltpu.roll(u, j, axis=1)
    rolled_tail = pltpu.roll(prev_tail, j, axis=1)
    t_idx = lax.broadcasted_iota(jnp.int32, (1, SUBLANES, 1), 1)
    head = jnp.where(t_idx < j, rolled_tail, rolled[:, :SUBLANES, :])
    return jnp.concatenate([head, rolled[:, SUBLANES:, :]], axis=1)


def _head_mean(sq_ref, pool):
    halves = [
        jnp.dot(sq_ref[:, k * MXU_DIM:(k + 1) * MXU_DIM], pool, preferred_element_type=jnp.float32)
        for k in range(sq_ref.shape[1] // MXU_DIM)
    ]
    return jnp.concatenate(halves, axis=1)


def _stream_from_hbm(sources, stage, sems):
    n_slots, stage_rows, stage_cols = stage.shape
    chunks = []
    for src, consume in sources:
        n_rows, n_cols = src.shape
        assert n_rows % stage_rows == 0 and n_cols % stage_cols == 0
        for r in range(0, n_rows, stage_rows):
            for c in range(0, n_cols, stage_cols):
                chunks.append((src, consume, r, c))

    def copy(i):
        src, _, r, c = chunks[i]
        slot = i % n_slots
        return pltpu.make_async_copy(src.at[pl.ds(r, stage_rows), pl.ds(c, stage_cols)],
                                     stage.at[slot], sems.at[slot])

    lookahead = n_slots - 1
    for i in range(min(lookahead, len(chunks))):
        copy(i).start()
    for i, (_, consume, r, c) in enumerate(chunks):
        if i + lookahead < len(chunks):
            copy(i + lookahead).start()
        copy(i).wait()
        consume(r, c, stage.at[i % n_slots])


def _block_kernel(x_ref, c_ref, ada_w_hbm, ada_b_ref, n1g_ref, w_in_hbm, lcw_ref, lcb_ref, wg_ref,
                  gab_ref, gxb_ref, ap_ref, scw_ref, log_ref, cog_ref, pool_ref, w_out_hbm, n2g_ref,
                  w1_hbm, w2_hbm, fg_ref, o_ref,
                  lx_tail, cv_tail, h_state, a_slab, b_slab, h_slab, hb_ref, x1_ref, h2_ref, hid_ref, mlp_ref,
                  xlb_ref, sql_ref, sqc_ref, yb_ref,
                  w_in_ref, w_out_ref, w1_ref, w2_ref, stage_ref, stage_sems, mod_acc, mod_ref):
    step = pl.program_id(0)
    bsz, tc, d = x_ref.shape
    rows = bsz * tc
    f32, bf16 = jnp.float32, jnp.bfloat16

    def run(mix, out, first):
        shift1, scale1, gate1, shift2, scale2, gate2 = [mod_ref[k] for k in range(N_ADA)]

        def mlp_out_piece(k):
            cols = slice(k * MXU_DIM, (k + 1) * MXU_DIM)
            mlp_ref[:, cols] = jnp.dot(hid_ref[...], w2_ref[:, cols], preferred_element_type=f32)

        def proj(k, width):
            u = jnp.dot(hb_ref[...], w_in_ref[:, k:k + width], preferred_element_type=f32)
            return u.reshape(bsz, tc, width)

        if out:
            mlp_out_piece(0)
        if not mix:
            for k in range(1, d // MXU_DIM):
                mlp_out_piece(k)
            x2 = x1_ref[...] + gate2 * mlp_ref[...].reshape(bsz, tc, d)
            o_ref[...] = x2 * _rms_scale(x2) * fg_ref[...][None]
            return
        x = x_ref[...]
        h = x * _rms_scale(x) * (n1g_ref[...][None] * (1.0 + scale1)) + shift1
        hb_ref[...] = h.reshape(rows, d).astype(bf16)

        u_lx = proj(0, D_LRU)
        u_c = proj(2 * D_LRU + D_CONV, D_CONV)
        u_v = proj(2 * D_LRU + 2 * D_CONV, D_CONV)
        lcw = lcw_ref[...]
        prev_lx = lx_tail[...]
        xl = u_lx * lcw[3][None, None, :] + lcb_ref[...][None]
        for j in range(1, 4):
            xl = xl + _time_shift(u_lx, prev_lx, j) * lcw[3 - j][None, None, :]
        lx_tail[...] = u_lx[:, tc - SUBLANES:, :]
        xlb_ref[...] = xl.reshape(rows, D_LRU).astype(bf16)
        scw = scw_ref[...]
        cv = u_c * u_v
        prev_cv = cv_tail[...]
        conv = cv * scw[2][None, None, :]
        for j in range(1, 3):
            conv = conv + _time_shift(cv, prev_cv, j) * scw[2 - j][None, None, :]
        cv_tail[...] = cv[:, tc - SUBLANES:, :]

        ap = ap_ref[...]
        softplus = jnp.maximum(ap, 0.0) + jnp.log1p(jnp.exp(-jnp.abs(ap)))
        half_rate = ((-0.5 * C_GATE * LOG2_E) * softplus)[None]
        half_ab = (0.5 * gab_ref[...])[None]
        half_xb = (0.5 * gxb_ref[...])[None]
        first_row = (lax.broadcasted_iota(jnp.int32, (1, SUBLANES, 1), 1) == 0) & (step == 0)

        def lru_inputs(k):
            ch = slice(k * MXU_DIM, (k + 1) * MXU_DIM)
            g = jnp.dot(xlb_ref[:, ch], wg_ref[k], preferred_element_type=f32)
            ga = g[:, :MXU_DIM].reshape(bsz, tc, MXU_DIM)
            gx = g[:, MXU_DIM:].reshape(bsz, tc, MXU_DIM)
            a = jnp.exp2(jnp.tanh(ga + half_ab[:, :, ch]) * half_rate[:, :, ch] + half_rate[:, :, ch])
            i = 0.5 * jnp.tanh(gx + half_xb[:, :, ch]) + 0.5
            v = 1.0 - a * a
            mult = jnp.where(v > 0.0, v * lax.rsqrt(v), 0.0)
            if first:
                mult_head = jnp.where(first_row, 1.0, mult[:, :SUBLANES, :])
                mult = jnp.concatenate([mult_head, mult[:, SUBLANES:, :]], axis=1)
            bx = mult * (i * xl[:, :, ch])
            for b in range(bsz):
                for j in range(MXU_DIM // LANES):
                    s = k * (MXU_DIM // LANES) + j
                    rows_b = pl.ds(b * SCAN_PITCH, tc)
                    a_slab[s, rows_b, :] = a[b, :, j * LANES:(j + 1) * LANES]
                    b_slab[s, rows_b, :] = bx[b, :, j * LANES:(j + 1) * LANES]

        lru_inputs(0)
        u_b = proj(2 * D_LRU, D_CONV)
        lru_inputs(1)
        u_ly = proj(D_LRU, D_LRU)

        if out:
            mlp_out_piece(1)
        y_c = (u_b * conv).reshape(rows, D_CONV)
        sqc_ref[...] = (y_c * y_c).astype(bf16)
        half_ly = 0.5 * u_ly
        gelu_ly = half_ly * jnp.tanh(u_ly * ((GELU_C0 * GELU_C1) * (u_ly * u_ly) + GELU_C0)) + half_ly
        pool = pool_ref[...]
        y_c = y_c * lax.rsqrt(_head_mean(sqc_ref, pool) + EPS) * cog_ref[...]
        yb_ref[:, D_LRU:] = y_c.astype(bf16)

        if out:
            mlp_out_piece(2)

        def scan_step(t, carry):
            new = []
            for s in range(N_SLABS):
                idx = pl.ds(t, bsz, stride=SCAN_PITCH)
                h_t = a_slab[s, idx, :] * carry[s] + b_slab[s, idx, :]
                h_slab[s, idx, :] = h_t
                new.append(h_t)
            return tuple(new)

        h0 = tuple(h_state[:, s * LANES:(s + 1) * LANES] for s in range(N_SLABS))
        h_last = lax.fori_loop(0, tc, scan_step, h0, unroll=True)
        for s in range(N_SLABS):
            h_state[:, s * LANES:(s + 1) * LANES] = h_last[s]
        hl = jnp.stack([
            jnp.concatenate([h_slab[s, pl.ds(b * SCAN_PITCH, tc), :] for s in range(N_SLABS)], axis=1)
            for b in range(bsz)], axis=0)
        y_l = (gelu_ly * hl).reshape(rows, D_LRU)
        sql_ref[...] = (y_l * y_l).astype(bf16)
        y_l = y_l * lax.rsqrt(_head_mean(sql_ref, pool) + EPS) * log_ref[...]
        yb_ref[:, :D_LRU] = y_l.astype(bf16)

        mixed = jnp.dot(yb_ref[...], w_out_ref[...], preferred_element_type=f32)
        if out:
            mlp_out_piece(3)
            x1_prev = x1_ref[...]
        x1 = x_ref[...] + gate1 * mixed.reshape(bsz, tc, d)
        x1_ref[...] = x1
        h2 = x1 * _rms_scale(x1) * (n2g_ref[...][None] * (1.0 + scale2)) + shift2
        h2_ref[...] = h2.reshape(rows, d).astype(bf16)

        for k in range(D_FF // FF_CHUNK):
            cols = slice(k * FF_CHUNK, (k + 1) * FF_CHUNK)
            hk = jnp.dot(h2_ref[...], w1_ref[:, cols], preferred_element_type=f32)
            hid_ref[:, cols] = jnp.square(jnp.maximum(hk, 0.0)).astype(bf16)
            if k == 0 and out:
                x2 = x1_prev + gate2 * mlp_ref[...].reshape(bsz, tc, d)
                o_ref[...] = x2 * _rms_scale(x2) * fg_ref[...][None]

    @pl.when(step == 0)
    def _():
        lx_tail[...] = jnp.zeros_like(lx_tail)
        cv_tail[...] = jnp.zeros_like(cv_tail)
        h_state[...] = jnp.zeros_like(h_state)
        x1_ref[...] = jnp.zeros_like(x1_ref)
        hid_ref[...] = jnp.zeros_like(hid_ref)

        sc = jax.nn.silu(c_ref[...]).astype(bf16)
        mod_acc[...] = jnp.broadcast_to(ada_b_ref[...], mod_acc.shape)

        def ada_chunk(r, c, chunk):
            part = jnp.dot(sc[:, r:r + chunk.shape[0]], chunk[...].astype(bf16),
                           preferred_element_type=f32)
            mod_acc[:, c:c + chunk.shape[1]] += part

        def cast_into(dst):
            def consume(r, c, chunk):
                dst[r:r + chunk.shape[0], c:c + chunk.shape[1]] = chunk[...].astype(dst.dtype)
            return consume

        _stream_from_hbm([(ada_w_hbm, ada_chunk), (w_in_hbm, cast_into(w_in_ref)),
                          (w_out_hbm, cast_into(w_out_ref)), (w1_hbm, cast_into(w1_ref)),
                          (w2_hbm, cast_into(w2_ref))], stage_ref, stage_sems)
        for k in range(N_ADA):
            for b in range(bsz):
                mod_ref[k, b] = mod_acc[b:b + 1, k * d:(k + 1) * d]

    run(mix=True, out=True, first=True)


def _block_diag(w):
    n, i, j = w.shape
    eye = jnp.eye(n, dtype=w.dtype)
    return (eye[:, None, :, None] * w[:, :, None, :]).reshape(n * i, n * j)


def _gate_weights(gate_a_w, gate_x_w):
    tiles = []
    for k in range(N_HEADS // HEADS_PER_MXU_TILE):
        hs = slice(k * HEADS_PER_MXU_TILE, (k + 1) * HEADS_PER_MXU_TILE)
        tiles.append(jnp.concatenate([_block_diag(gate_a_w[hs]), _block_diag(gate_x_w[hs])], axis=1))
    return (0.5 * jnp.stack(tiles)).astype(jnp.bfloat16)


def _const_spec(shape):
    if shape is None:
        return pl.BlockSpec(memory_space=pl.ANY)
    return pl.BlockSpec(shape, lambda s: (0,) * len(shape), pipeline_mode=pl.Buffered(1))


@jax.jit
def kernel(x, c, ada_w, ada_b, norm1_g, w_in, lru_conv_w, lru_conv_b, gate_a_w, gate_a_b, gate_x_w,
           gate_x_b, a_param, short_conv_w, lru_out_g, conv_out_g, w_out, norm2_g, w_mlp1, w_mlp2,
           final_g):
    bsz, seq, d = x.shape
    assert (d, seq % TIME_CHUNK, bsz) == (D_MODEL, 0, SUBLANES)
    assert ada_w.shape[0] == 1, "one layer"
    pool = _block_diag(jnp.full((HEADS_PER_MXU_TILE, HEAD_DIM, HEAD_DIM), 1.0 / HEAD_DIM, jnp.bfloat16))

    operands = [
        (c, (bsz, d)),
        (ada_w.reshape(d, N_ADA * d), None),
        (ada_b, (1, N_ADA * d)),
        (norm1_g, (1, d)),
        (w_in.reshape(d, D_IN), None),
        (lru_conv_w.reshape(4, D_LRU), (4, D_LRU)),
        (lru_conv_b, (1, D_LRU)),
        (_gate_weights(gate_a_w[0], gate_x_w[0]), (D_LRU // MXU_DIM, MXU_DIM, 2 * MXU_DIM)),
        (gate_a_b, (1, D_LRU)),
        (gate_x_b, (1, D_LRU)),
        (a_param, (1, D_LRU)),
        (short_conv_w.reshape(3, D_CONV), (3, D_CONV)),
        (lru_out_g, (1, D_LRU)),
        (conv_out_g, (1, D_CONV)),
        (pool, (MXU_DIM, MXU_DIM)),
        (w_out.reshape(D_LRU + D_CONV, d), None),
        (norm2_g, (1, d)),
        (w_mlp1.reshape(d, D_FF), None),
        (w_mlp2.reshape(D_FF, d), None),
        (final_g.reshape(1, d), (1, d)),
    ]
    n_chunks = seq // TIME_CHUNK
    block = (bsz, TIME_CHUNK, d)
    rows = bsz * TIME_CHUNK
    return pl.pallas_call(
        _block_kernel,
        grid=(n_chunks + 1,),
        in_specs=[pl.BlockSpec(block, lambda s: (0, jnp.minimum(s, n_chunks - 1), 0))]
        + [_const_spec(shape) for _, shape in operands],
        out_specs=pl.BlockSpec(block, lambda s: (0, jnp.maximum(s - 1, 0), 0)),
        out_shape=jax.ShapeDtypeStruct(x.shape, x.dtype),
        scratch_shapes=[
            pltpu.VMEM((bsz, SUBLANES, D_LRU), jnp.float32),
            pltpu.VMEM((bsz, SUBLANES, D_CONV), jnp.float32),
            pltpu.VMEM((bsz, D_LRU), jnp.float32),
            pltpu.VMEM((N_SLABS, bsz * SCAN_PITCH, LANES), jnp.float32),
            pltpu.VMEM((N_SLABS, bsz * SCAN_PITCH, LANES), jnp.float32),
            pltpu.VMEM((N_SLABS, bsz * SCAN_PITCH, LANES), jnp.float32),
            pltpu.VMEM((rows, d), jnp.bfloat16),
            pltpu.VMEM(block, jnp.float32),
            pltpu.VMEM((rows, d), jnp.bfloat16),
            pltpu.VMEM((rows, D_FF), jnp.bfloat16),
            pltpu.VMEM((rows, d), jnp.float32),
            pltpu.VMEM((rows, D_LRU), jnp.bfloat16),
            pltpu.VMEM((rows, D_LRU), jnp.bfloat16),
            pltpu.VMEM((rows, D_CONV), jnp.bfloat16),
            pltpu.VMEM((rows, D_LRU + D_CONV), jnp.bfloat16),
            pltpu.VMEM((d, D_IN), jnp.bfloat16),
            pltpu.VMEM((D_LRU + D_CONV, d), jnp.bfloat16),
            pltpu.VMEM((d, D_FF), jnp.bfloat16),
            pltpu.VMEM((D_FF, d), jnp.bfloat16),
            pltpu.VMEM((STAGE_SLOTS, STAGE_ROWS, STAGE_COLS), jnp.float32),
            pltpu.SemaphoreType.DMA((STAGE_SLOTS,)),
            pltpu.VMEM((bsz, N_ADA * d), jnp.float32),
            pltpu.VMEM((N_ADA, bsz, 1, d), jnp.float32),
        ],
        compiler_params=pltpu.CompilerParams(
            dimension_semantics=("arbitrary",), vmem_limit_bytes=VMEM_LIMIT_BYTES),
        name="hybrid_block",
    )(x, *[v for v, _ in operands])
```

```python
import jax
import jax.numpy as jnp
import numpy as np
from jax import lax
from jax.experimental import pallas as pl
from jax.experimental.pallas import tpu as pltpu

D_MODEL = 1024
D_LRU = 512
D_CONV = 512
N_HEADS = 8
HEAD_DIM = 64
D_FF = 4096
D_IN = 2 * D_LRU + 3 * D_CONV
N_ADA = 6
C_GATE = 8.0
EPS = 1e-6
LOG2_E = 1.4426950408889634
GELU_C0 = 0.7978845608028654
GELU_C1 = 0.044715

LANES = 128
SUBLANES = 8
MXU_DIM = 256
HEADS_PER_MXU_TILE = MXU_DIM // HEAD_DIM

TIME_CHUNK = 64
FF_CHUNK = 1024
STAGE_ROWS, STAGE_COLS = 256, 512
STAGE_SLOTS = 8
SCAN_PITCH = TIME_CHUNK + SUBLANES
N_SLABS = D_LRU // LANES

VMEM_LIMIT_BYTES = 56 * 1024 * 1024


def _rms_scale(v):
    return lax.rsqrt(jnp.mean(v * v, axis=-1, keepdims=True) + EPS)


def _time_shift(u, prev_tail, j):
    rolled = pltpu.roll(u, j, axis=1)
    rolled_tail = pltpu.roll(prev_tail, j, axis=1)
    t_idx = lax.broadcasted_iota(jnp.int32, (1, SUBLANES, 1), 1)
    head = jnp.where(t_idx < j, rolled_tail, rolled[:, :SUBLANES, :])
    return jnp.concatenate([head, rolled[:, SUBLANES:, :]], axis=1)


def _head_mean(sq_ref, pool):
    halves = [
        jnp.dot(sq_ref[:, k * MXU_DIM:(k + 1) * MXU_DIM], pool, preferred_element_type=jnp.float32)
        for k in range(sq_ref.shape[1] // MXU_DIM)
    ]
    return jnp.concatenate(halves, axis=1)


def _stream_from_hbm(sources, stage, sems):
    n_slots, stage_rows, stage_cols = stage.shape
    chunks = []
    for src, consume in sources:
        n_rows, n_cols = src.shape
        assert n_rows % stage_rows == 0 and n_cols % stage_cols == 0
        for r in range(0, n_rows, stage_rows):
            for c in range(0, n_cols, stage_cols):
                chunks.append((src, consume, r, c))

    def copy(i):
        src, _, r, c = chunks[i]
        slot = i % n_slots
        return pltpu.make_async_copy(src.at[pl.ds(r, stage_rows), pl.ds(c, stage_cols)],
                                     stage.at[slot], sems.at[slot])

    lookahead = n_slots - 1
    for i in range(min(lookahead, len(chunks))):
        copy(i).start()
    for i, (_, consume, r, c) in enumerate(chunks):
        if i + lookahead < len(chunks):
            copy(i + lookahead).start()
        copy(i).wait()
        consume(r, c, stage.at[i % n_slots])


def _block_kernel(x_ref, c_ref, ada_w_hbm, ada_b_ref, n1g_ref, w_in_hbm, lcw_ref, lcb_ref, wg_ref,
                  gab_ref, gxb_ref, ap_ref, scw_ref, log_ref, cog_ref, pool_ref, w_out_hbm, n2g_ref,
                  w1_hbm, w2_hbm, fg_ref, o_ref,
                  lx_tail, cv_tail, h_state, a_slab, b_slab, h_slab, hb_ref, x1_ref, h2_ref, hid_ref, mlp_ref,
                  xlb_ref, sql_ref, sqc_ref, yb_ref,
                  w_in_ref, w_out_ref, w1_ref, w2_ref, stage_ref, stage_sems, mod_acc, mod_ref):
    step = pl.program_id(0)
    bsz, tc, d = x_ref.shape
    rows = bsz * tc
    f32, bf16 = jnp.float32, jnp.bfloat16

    def run(mix, out, first):
        shift1, scale1, gate1, shift2, scale2, gate2 = [mod_ref[k] for k in range(N_ADA)]

        def mlp_out_piece(k):
            cols = slice(k * MXU_DIM, (k + 1) * MXU_DIM)
            mlp_ref[:, cols] = jnp.dot(hid_ref[...], w2_ref[:, cols], preferred_element_type=f32)

        def proj(k, width):
            u = jnp.dot(hb_ref[...], w_in_ref[:, k:k + width], preferred_element_type=f32)
            return u.reshape(bsz, tc, width)

        if out:
            mlp_out_piece(0)
        if not mix:
            for k in range(1, d // MXU_DIM):
                mlp_out_piece(k)
            x2 = x1_ref[...] + gate2 * mlp_ref[...].reshape(bsz, tc, d)
            o_ref[...] = x2 * _rms_scale(x2) * fg_ref[...][None]
            return
        x = x_ref[...]
        h = x * _rms_scale(x) * (n1g_ref[...][None] * (1.0 + scale1)) + shift1
        hb_ref[...] = h.reshape(rows, d).astype(bf16)

        u_lx = proj(0, D_LRU)
        u_c = proj(2 * D_LRU + D_CONV, D_CONV)
        u_v = proj(2 * D_LRU + 2 * D_CONV, D_CONV)
        lcw = lcw_ref[...]
        prev_lx = lx_tail[...]
        xl = u_lx * lcw[3][None, None, :] + lcb_ref[...][None]
        for j in range(1, 4):
            xl = xl + _time_shift(u_lx, prev_lx, j) * lcw[3 - j][None, None, :]
        lx_tail[...] = u_lx[:, tc - SUBLANES:, :]
        xlb_ref[...] = xl.reshape(rows, D_LRU).astype(bf16)
        scw = scw_ref[...]
        cv = u_c * u_v
        prev_cv = cv_tail[...]
        conv = cv * scw[2][None, None, :]
        for j in range(1, 3):
            conv = conv + _time_shift(cv, prev_cv, j) * scw[2 - j][None, None, :]
        cv_tail[...] = cv[:, tc - SUBLANES:, :]

        ap = ap_ref[...]
        softplus = jnp.maximum(ap, 0.0) + jnp.log1p(jnp.exp(-jnp.abs(ap)))
        half_rate = ((-0.5 * C_GATE * LOG2_E) * softplus)[None]
        half_ab = (0.5 * gab_ref[...])[None]
        half_xb = (0.5 * gxb_ref[...])[None]
        first_row = (lax.broadcasted_iota(jnp.int32, (1, SUBLANES, 1), 1) == 0) & (step == 0)

        def lru_inputs(k):
            ch = slice(k * MXU_DIM, (k + 1) * MXU_DIM)
            g = jnp.dot(xlb_ref[:, ch], wg_ref[k], preferred_element_type=f32)
            ga = g[:, :MXU_DIM].reshape(bsz, tc, MXU_DIM)
            gx = g[:, MXU_DIM:].reshape(bsz, tc, MXU_DIM)
            a = jnp.exp2(jnp.tanh(ga + half_ab[:, :, ch]) * half_rate[:, :, ch] + half_rate[:, :, ch])
            i = 0.5 * jnp.tanh(gx + half_xb[:, :, ch]) + 0.5
            v = 1.0 - a * a
            mult = jnp.where(v > 0.0, v * lax.rsqrt(v), 0.0)
            if first:
                mult_head = jnp.where(first_row, 1.0, mult[:, :SUBLANES, :])
                mult = jnp.concatenate([mult_head, mult[:, SUBLANES:, :]], axis=1)
            bx = mult * (i * xl[:, :, ch])
            for b in range(bsz):
                for j in range(MXU_DIM // LANES):
                    s = k * (MXU_DIM // LANES) + j
                    rows_b = pl.ds(b * SCAN_PITCH, tc)
                    a_slab[s, rows_b, :] = a[b, :, j * LANES:(j + 1) * LANES]
                    b_slab[s, rows_b, :] = bx[b, :, j * LANES:(j + 1) * LANES]

        lru_inputs(0)
        u_b = proj(2 * D_LRU, D_CONV)
        lru_inputs(1)
        u_ly = proj(D_LRU, D_LRU)

        if out:
            mlp_out_piece(1)
        y_c = (u_b * conv).reshape(rows, D_CONV)
        sqc_ref[...] = (y_c * y_c).astype(bf16)
        half_ly = 0.5 * u_ly
        gelu_ly = half_ly * jnp.tanh(u_ly * ((GELU_C0 * GELU_C1) * (u_ly * u_ly) + GELU_C0)) + half_ly
        pool = pool_ref[...]
        y_c = y_c * lax.rsqrt(_head_mean(sqc_ref, pool) + EPS) * cog_ref[...]
        yb_ref[:, D_LRU:] = y_c.astype(bf16)

        if out:
            mlp_out_piece(2)

        def scan_step(t, carry):
            new = []
            for s in range(N_SLABS):
                idx = pl.ds(t, bsz, stride=SCAN_PITCH)
                h_t = a_slab[s, idx, :] * carry[s] + b_slab[s, idx, :]
                h_slab[s, idx, :] = h_t
                new.append(h_t)
            return tuple(new)

        h0 = tuple(h_state[:, s * LANES:(s + 1) * LANES] for s in range(N_SLABS))
        h_last = lax.fori_loop(0, tc, scan_step, h0, unroll=True)
        for s in range(N_SLABS):
            h_state[:, s * LANES:(s + 1) * LANES] = h_last[s]
        hl = jnp.stack([
            jnp.concatenate([h_slab[s, pl.ds(b * SCAN_PITCH, tc), :] for s in range(N_SLABS)], axis=1)
            for b in range(bsz)], axis=0)
        y_l = (gelu_ly * hl).reshape(rows, D_LRU)
        sql_ref[...] = (y_l * y_l).astype(bf16)
        y_l = y_l * lax.rsqrt(_head_mean(sql_ref, pool) + EPS) * log_ref[...]
        yb_ref[:, :D_LRU] = y_l.astype(bf16)

        mixed = jnp.dot(yb_ref[...], w_out_ref[...], preferred_element_type=f32)
        if out:
            mlp_out_piece(3)
            x1_prev = x1_ref[...]
        x1 = x_ref[...] + gate1 * mixed.reshape(bsz, tc, d)
        x1_ref[...] = x1
        h2 = x1 * _rms_scale(x1) * (n2g_ref[...][None] * (1.0 + scale2)) + shift2
        h2_ref[...] = h2.reshape(rows, d).astype(bf16)

        for k in range(D_FF // FF_CHUNK):
            cols = slice(k * FF_CHUNK, (k + 1) * FF_CHUNK)
            hk = jnp.dot(h2_ref[...], w1_ref[:, cols], preferred_element_type=f32)
            hid_ref[:, cols] = jnp.square(jnp.maximum(hk, 0.0)).astype(bf16)
            if k == 0 and out:
                x2 = x1_prev + gate2 * mlp_ref[...].reshape(bsz, tc, d)
                o_ref[...] = x2 * _rms_scale(x2) * fg_ref[...][None]

    @pl.when(step == 0)
    def _():
        lx_tail[...] = jnp.zeros_like(lx_tail)
        cv_tail[...] = jnp.zeros_like(cv_tail)
        h_state[...] = jnp.zeros_like(h_state)
        x1_ref[...] = jnp.zeros_like(x1_ref)
        hid_ref[...] = jnp.zeros_like(hid_ref)

        sc = jax.nn.silu(c_ref[...]).astype(bf16)
        mod_acc[...] = jnp.broadcast_to(ada_b_ref[...], mod_acc.shape)

        def ada_chunk(r, c, chunk):
            part = jnp.dot(sc[:, r:r + chunk.shape[0]], chunk[...].astype(bf16),
                           preferred_element_type=f32)
            mod_acc[:, c:c + chunk.shape[1]] += part

        def cast_into(dst):
            def consume(r, c, chunk):
                dst[r:r + chunk.shape[0], c:c + chunk.shape[1]] = chunk[...].astype(dst.dtype)
            return consume

        _stream_from_hbm([(ada_w_hbm, ada_chunk), (w_in_hbm, cast_into(w_in_ref)),
                          (w_out_hbm, cast_into(w_out_ref)), (w1_hbm, cast_into(w1_ref)),
                          (w2_hbm, cast_into(w2_ref))], stage_ref, stage_sems)
        for k in range(N_ADA):
            for b in range(bsz):
                mod_ref[k, b] = mod_acc[b:b + 1, k * d:(k + 1) * d]

    run(mix=True, out=True, first=True)


def _gate_weights(gate_a_w, gate_x_w):
    n_tiles, hpt = N_HEADS // HEADS_PER_MXU_TILE, HEADS_PER_MXU_TILE
    w = jnp.stack([gate_a_w.reshape(n_tiles, hpt, HEAD_DIM, HEAD_DIM),
                   gate_x_w.reshape(n_tiles, hpt, HEAD_DIM, HEAD_DIM)], axis=1)
    eye = jnp.eye(hpt, dtype=w.dtype)
    blocks = w[:, :, :, :, None, :] * eye[None, None, :, None, :, None]
    blocks = blocks.reshape(n_tiles, 2, MXU_DIM, MXU_DIM).transpose(0, 2, 1, 3)
    return (0.5 * blocks.reshape(n_tiles, MXU_DIM, 2 * MXU_DIM)).astype(jnp.bfloat16)


def _head_pool_matrix():
    blocks = np.kron(np.eye(HEADS_PER_MXU_TILE), np.full((HEAD_DIM, HEAD_DIM), 1.0 / HEAD_DIM))
    return jnp.asarray(blocks, dtype=jnp.bfloat16)


def _const_spec(shape):
    if shape is None:
        return pl.BlockSpec(memory_space=pl.ANY)
    return pl.BlockSpec(shape, lambda s: (0,) * len(shape), pipeline_mode=pl.Buffered(1))


@jax.jit
def kernel(x, c, ada_w, ada_b, norm1_g, w_in, lru_conv_w, lru_conv_b, gate_a_w, gate_a_b, gate_x_w,
           gate_x_b, a_param, short_conv_w, lru_out_g, conv_out_g, w_out, norm2_g, w_mlp1, w_mlp2,
           final_g):
    bsz, seq, d = x.shape
    assert (d, seq % TIME_CHUNK, bsz) == (D_MODEL, 0, SUBLANES)
    assert ada_w.shape[0] == 1, "one layer"

    operands = [
        (c, (bsz, d)),
        (ada_w.reshape(d, N_ADA * d), None),
        (ada_b, (1, N_ADA * d)),
        (norm1_g, (1, d)),
        (w_in.reshape(d, D_IN), None),
        (lru_conv_w.reshape(4, D_LRU), (4, D_LRU)),
        (lru_conv_b, (1, D_LRU)),
        (_gate_weights(gate_a_w, gate_x_w), (D_LRU // MXU_DIM, MXU_DIM, 2 * MXU_DIM)),
        (gate_a_b, (1, D_LRU)),
        (gate_x_b, (1, D_LRU)),
        (a_param, (1, D_LRU)),
        (short_conv_w.reshape(3, D_CONV), (3, D_CONV)),
        (lru_out_g, (1, D_LRU)),
        (conv_out_g, (1, D_CONV)),
        (_head_pool_matrix(), (MXU_DIM, MXU_DIM)),
        (w_out.reshape(D_LRU + D_CONV, d), None),
        (norm2_g, (1, d)),
        (w_mlp1.reshape(d, D_FF), None),
        (w_mlp2.reshape(D_FF, d), None),
        (final_g.reshape(1, d), (1, d)),
    ]
    n_chunks = seq // TIME_CHUNK
    block = (bsz, TIME_CHUNK, d)
    rows = bsz * TIME_CHUNK
    return pl.pallas_call(
        _block_kernel,
        grid=(n_chunks + 1,),
        in_specs=[pl.BlockSpec(block, lambda s: (0, jnp.minimum(s, n_chunks - 1), 0))]
        + [_const_spec(shape) for _, shape in operands],
        out_specs=pl.BlockSpec(block, lambda s: (0, jnp.maximum(s - 1, 0), 0)),
        out_shape=jax.ShapeDtypeStruct(x.shape, x.dtype),
        scratch_shapes=[
            pltpu.VMEM((bsz, SUBLANES, D_LRU), jnp.float32),
            pltpu.VMEM((bsz, SUBLANES, D_CONV), jnp.float32),
            pltpu.VMEM((bsz, D_LRU), jnp.float32),
            pltpu.VMEM((N_SLABS, bsz * SCAN_PITCH, LANES), jnp.float32),
            pltpu.VMEM((N_SLABS, bsz * SCAN_PITCH, LANES), jnp.float32),
            pltpu.VMEM((N_SLABS, bsz * SCAN_PITCH, LANES), jnp.float32),
            pltpu.VMEM((rows, d), jnp.bfloat16),
            pltpu.VMEM(block, jnp.float32),
            pltpu.VMEM((rows, d), jnp.bfloat16),
            pltpu.VMEM((rows, D_FF), jnp.bfloat16),
            pltpu.VMEM((rows, d), jnp.float32),
            pltpu.VMEM((rows, D_LRU), jnp.bfloat16),
            pltpu.VMEM((rows, D_LRU), jnp.bfloat16),
            pltpu.VMEM((rows, D_CONV), jnp.bfloat16),
            pltpu.VMEM((rows, D_LRU + D_CONV), jnp.bfloat16),
            pltpu.VMEM((d, D_IN), jnp.bfloat16),
            pltpu.VMEM((D_LRU + D_CONV, d), jnp.bfloat16),
            pltpu.VMEM((d, D_FF), jnp.bfloat16),
            pltpu.VMEM((D_FF, d), jnp.bfloat16),
            pltpu.VMEM((STAGE_SLOTS, STAGE_ROWS, STAGE_COLS), jnp.float32),
            pltpu.SemaphoreType.DMA((STAGE_SLOTS,)),
            pltpu.VMEM((bsz, N_ADA * d), jnp.float32),
            pltpu.VMEM((N_ADA, bsz, 1, d), jnp.float32),
        ],
        compiler_params=pltpu.CompilerParams(
            dimension_semantics=("arbitrary",), vmem_limit_bytes=VMEM_LIMIT_BYTES),
        name="hybrid_block",
    )(x, *[v for v, _ in operands])
```

```python
import jax
import jax.numpy as jnp
import numpy as np
from jax import lax
from jax.experimental import pallas as pl
from jax.experimental.pallas import tpu as pltpu

D_MODEL = 1024
D_LRU = 512
D_CONV = 512
N_HEADS = 8
HEAD_DIM = 64
D_FF = 4096
D_IN = 2 * D_LRU + 3 * D_CONV
N_ADA = 6
C_GATE = 8.0
EPS = 1e-6
LOG2_E = 1.4426950408889634
GELU_C0 = 0.7978845608028654
GELU_C1 = 0.044715

LANES = 128
SUBLANES = 8
MXU_DIM = 256
HEADS_PER_MXU_TILE = MXU_DIM // HEAD_DIM

TIME_CHUNK = 64
FF_CHUNK = 1024
STAGE_ROWS, STAGE_COLS = 256, 512
STAGE_SLOTS = 8
SCAN_PITCH = TIME_CHUNK + SUBLANES
N_SLABS = D_LRU // LANES

VMEM_LIMIT_BYTES = 56 * 1024 * 1024


def _rms_scale(v):
    return lax.rsqrt(jnp.mean(v * v, axis=-1, keepdims=True) + EPS)


def _time_shift(u, prev_tail, j):
    rolled = pltpu.roll(u, j, axis=1)
    rolled_tail = pltpu.roll(prev_tail, j, axis=1)
    t_idx = lax.broadcasted_iota(jnp.int32, (1, SUBLANES, 1), 1)
    head = jnp.where(t_idx < j, rolled_tail, rolled[:, :SUBLANES, :])
    return jnp.concatenate([head, rolled[:, SUBLANES:, :]], axis=1)


def _head_mean(sq_ref, pool):
    halves = [
        jnp.dot(sq_ref[:, k * MXU_DIM:(k + 1) * MXU_DIM], pool, preferred_element_type=jnp.float32)
        for k in range(sq_ref.shape[1] // MXU_DIM)
    ]
    return jnp.concatenate(halves, axis=1)


def _stream_from_hbm(sources, stage, sems):
    n_slots, stage_rows, stage_cols = stage.shape
    chunks = []
    for src, consume in sources:
        n_rows, n_cols = src.shape
        assert n_rows % stage_rows == 0 and n_cols % stage_cols == 0
        for r in range(0, n_rows, stage_rows):
            for c in range(0, n_cols, stage_cols):
                chunks.append((src, consume, r, c))

    def copy(i):
        src, _, r, c = chunks[i]
        slot = i % n_slots
        return pltpu.make_async_copy(src.at[pl.ds(r, stage_rows), pl.ds(c, stage_cols)],
                                     stage.at[slot], sems.at[slot])

    lookahead = n_slots - 1
    for i in range(min(lookahead, len(chunks))):
        copy(i).start()
    for i, (_, consume, r, c) in enumerate(chunks):
        if i + lookahead < len(chunks):
            copy(i + lookahead).start()
        copy(i).wait()
        consume(r, c, stage.at[i % n_slots])


def _block_kernel(x_ref, c_ref, ada_w_hbm, ada_b_ref, n1g_ref, w_in_hbm, lcw_ref, lcb_ref, wg_ref,
                  gab_ref, gxb_ref, ap_ref, scw_ref, log_ref, cog_ref, pool_ref, w_out_hbm, n2g_ref,
                  w1_hbm, w2_hbm, fg_ref, o_ref,
                  lx_tail, cv_tail, h_state, a_slab, b_slab, h_slab, hb_ref, x1_ref, h2_ref, hid_ref, mlp_ref,
                  xlb_ref, sql_ref, sqc_ref, yb_ref,
                  w_in_ref, w_out_ref, w1_ref, w2_ref, stage_ref, stage_sems, mod_acc, mod_ref):
    step = pl.program_id(0)
    bsz, tc, d = x_ref.shape
    rows = bsz * tc
    f32, bf16 = jnp.float32, jnp.bfloat16

    def run(mix, out, first):
        shift1, scale1, gate1, shift2, scale2, gate2 = [mod_ref[k] for k in range(N_ADA)]

        def mlp_out_piece(k, k_half=None):
            cols = slice(k * MXU_DIM, (k + 1) * MXU_DIM)
            if k_half is None:
                mlp_ref[:, cols] = jnp.dot(hid_ref[...], w2_ref[:, cols], preferred_element_type=f32)
                return
            feats = slice(k_half * (D_FF // 2), (k_half + 1) * (D_FF // 2))
            part = jnp.dot(hid_ref[:, feats], w2_ref[feats, cols], preferred_element_type=f32)
            if k_half == 0:
                mlp_ref[:, cols] = part
            else:
                mlp_ref[:, cols] += part

        def proj(k, width):
            u = jnp.dot(hb_ref[...], w_in_ref[:, k:k + width], preferred_element_type=f32)
            return u.reshape(bsz, tc, width)

        if out:
            mlp_out_piece(0)
        if not mix:
            for k in range(1, d // MXU_DIM):
                mlp_out_piece(k)
            x2 = x1_ref[...] + gate2 * mlp_ref[...].reshape(bsz, tc, d)
            o_ref[...] = x2 * _rms_scale(x2) * fg_ref[...][None]
            return
        x = x_ref[...]
        h = x * _rms_scale(x) * (n1g_ref[...][None] * (1.0 + scale1)) + shift1
        hb_ref[...] = h.reshape(rows, d).astype(bf16)

        u_lx = proj(0, D_LRU)
        u_c = proj(2 * D_LRU + D_CONV, D_CONV)
        u_v = proj(2 * D_LRU + 2 * D_CONV, D_CONV)
        lcw = lcw_ref[...]
        prev_lx = lx_tail[...]
        xl = u_lx * lcw[3][None, None, :] + lcb_ref[...][None]
        for j in range(1, 4):
            xl = xl + _time_shift(u_lx, prev_lx, j) * lcw[3 - j][None, None, :]
        lx_tail[...] = u_lx[:, tc - SUBLANES:, :]
        xlb_ref[...] = xl.reshape(rows, D_LRU).astype(bf16)
        scw = scw_ref[...]
        cv = u_c * u_v
        prev_cv = cv_tail[...]
        conv = cv * scw[2][None, None, :]
        for j in range(1, 3):
            conv = conv + _time_shift(cv, prev_cv, j) * scw[2 - j][None, None, :]
        cv_tail[...] = cv[:, tc - SUBLANES:, :]

        ap = ap_ref[...]
        softplus = jnp.maximum(ap, 0.0) + jnp.log1p(jnp.exp(-jnp.abs(ap)))
        half_rate = ((-0.5 * C_GATE * LOG2_E) * softplus)[None]
        half_ab = (0.5 * gab_ref[...])[None]
        half_xb = (0.5 * gxb_ref[...])[None]
        first_row = (lax.broadcasted_iota(jnp.int32, (1, SUBLANES, 1), 1) == 0) & (step == 0)

        def lru_inputs(k):
            ch = slice(k * MXU_DIM, (k + 1) * MXU_DIM)
            g = jnp.dot(xlb_ref[:, ch], wg_ref[k], preferred_element_type=f32)
            ga = g[:, :MXU_DIM].reshape(bsz, tc, MXU_DIM)
            gx = g[:, MXU_DIM:].reshape(bsz, tc, MXU_DIM)
            a = jnp.exp2(jnp.tanh(ga + half_ab[:, :, ch]) * half_rate[:, :, ch] + half_rate[:, :, ch])
            i = 0.5 * jnp.tanh(gx + half_xb[:, :, ch]) + 0.5
            v = 1.0 - a * a
            mult = jnp.where(v > 0.0, v * lax.rsqrt(v), 0.0)
            if first:
                mult_head = jnp.where(first_row, 1.0, mult[:, :SUBLANES, :])
                mult = jnp.concatenate([mult_head, mult[:, SUBLANES:, :]], axis=1)
            bx = mult * (i * xl[:, :, ch])
            for b in range(bsz):
                for j in range(MXU_DIM // LANES):
                    s = k * (MXU_DIM // LANES) + j
                    rows_b = pl.ds(b * SCAN_PITCH, tc)
                    a_slab[s, rows_b, :] = a[b, :, j * LANES:(j + 1) * LANES]
                    b_slab[s, rows_b, :] = bx[b, :, j * LANES:(j + 1) * LANES]

        lru_inputs(0)
        u_b = proj(2 * D_LRU, D_CONV)
        if out:
            mlp_out_piece(1, 0)
        lru_inputs(1)
        u_ly = proj(D_LRU, D_LRU)
        if out:
            mlp_out_piece(1, 1)
        y_c =(u_b * conv).reshape(rows, D_CONV)
        sqc_ref[...] = (y_c * y_c).astype(bf16)
        half_ly = 0.5 * u_ly
        gelu_ly = half_ly * jnp.tanh(u_ly * ((GELU_C0 * GELU_C1) * (u_ly * u_ly) + GELU_C0)) + half_ly
        pool = pool_ref[...]
        y_c = y_c * lax.rsqrt(_head_mean(sqc_ref, pool) + EPS) * cog_ref[...]
        yb_ref[:, D_LRU:] = y_c.astype(bf16)

        if out:
            mlp_out_piece(2)

        def scan_step(t, carry):
            new = []
            for s in range(N_SLABS):
                idx = pl.ds(t, bsz, stride=SCAN_PITCH)
                h_t = a_slab[s, idx, :] * carry[s] + b_slab[s, idx, :]
                h_slab[s, idx, :] = h_t
                new.append(h_t)
            return tuple(new)

        h0 = tuple(h_state[:, s * LANES:(s + 1) * LANES] for s in range(N_SLABS))
        h_last = lax.fori_loop(0, tc, scan_step, h0, unroll=True)
        for s in range(N_SLABS):
            h_state[:, s * LANES:(s + 1) * LANES] = h_last[s]
        hl = jnp.stack([
            jnp.concatenate([h_slab[s, pl.ds(b * SCAN_PITCH, tc), :] for s in range(N_SLABS)], axis=1)
            for b in range(bsz)], axis=0)
        y_l = (gelu_ly * hl).reshape(rows, D_LRU)
        sql_ref[...] = (y_l * y_l).astype(bf16)
        y_l = y_l * lax.rsqrt(_head_mean(sql_ref, pool) + EPS) * log_ref[...]
        yb_ref[:, :D_LRU] = y_l.astype(bf16)

        mixed = jnp.dot(yb_ref[...], w_out_ref[...], preferred_element_type=f32)
        if out:
            mlp_out_piece(3)
            x1_prev = x1_ref[...]
        x1 = x_ref[...] + gate1 * mixed.reshape(bsz, tc, d)
        x1_ref[...] = x1
        h2 = x1 * _rms_scale(x1) * (n2g_ref[...][None] * (1.0 + scale2)) + shift2
        h2_ref[...] = h2.reshape(rows, d).astype(bf16)

        for k in range(D_FF // FF_CHUNK):
            cols = slice(k * FF_CHUNK, (k + 1) * FF_CHUNK)
            hk = jnp.dot(h2_ref[...], w1_ref[:, cols], preferred_element_type=f32)
            hid_ref[:, cols] = jnp.square(jnp.maximum(hk, 0.0)).astype(bf16)
            if k == 0 and out:
                x2 = x1_prev + gate2 * mlp_ref[...].reshape(bsz, tc, d)
                o_ref[...] = x2 * _rms_scale(x2) * fg_ref[...][None]

    @pl.when(step == 0)
    def _():
        lx_tail[...] = jnp.zeros_like(lx_tail)
        cv_tail[...] = jnp.zeros_like(cv_tail)
        h_state[...] = jnp.zeros_like(h_state)
        x1_ref[...] = jnp.zeros_like(x1_ref)
        hid_ref[...] = jnp.zeros_like(hid_ref)

        sc = jax.nn.silu(c_ref[...]).astype(bf16)
        mod_acc[...] = jnp.broadcast_to(ada_b_ref[...], mod_acc.shape)

        def ada_chunk(r, c, chunk):
            part = jnp.dot(sc[:, r:r + chunk.shape[0]], chunk[...].astype(bf16),
                           preferred_element_type=f32)
            mod_acc[:, c:c + chunk.shape[1]] += part

        def cast_into(dst):
            def consume(r, c, chunk):
                dst[r:r + chunk.shape[0], c:c + chunk.shape[1]] = chunk[...].astype(dst.dtype)
            return consume

        _stream_from_hbm([(ada_w_hbm, ada_chunk), (w_in_hbm, cast_into(w_in_ref)),
                          (w_out_hbm, cast_into(w_out_ref)), (w1_hbm, cast_into(w1_ref)),
                          (w2_hbm, cast_into(w2_ref))], stage_ref, stage_sems)
        for k in range(N_ADA):
            for b in range(bsz):
                mod_ref[k, b] = mod_acc[b:b + 1, k * d:(k + 1) * d]

    run(mix=True, out=True, first=True)


def _gate_weights(gate_a_w, gate_x_w):
    n_tiles, hpt = N_HEADS // HEADS_PER_MXU_TILE, HEADS_PER_MXU_TILE
    w = jnp.stack([gate_a_w.reshape(n_tiles, hpt, HEAD_DIM, HEAD_DIM),
                   gate_x_w.reshape(n_tiles, hpt, HEAD_DIM, HEAD_DIM)], axis=1)
    eye = jnp.eye(hpt, dtype=w.dtype)
    blocks = w[:, :, :, :, None, :] * eye[None, None, :, None, :, None]
    blocks = blocks.reshape(n_tiles, 2, MXU_DIM, MXU_DIM).transpose(0, 2, 1, 3)
    return (0.5 * blocks.reshape(n_tiles, MXU_DIM, 2 * MXU_DIM)).astype(jnp.bfloat16)


def _head_pool_matrix():
    blocks = np.kron(np.eye(HEADS_PER_MXU_TILE), np.full((HEAD_DIM, HEAD_DIM), 1.0 / HEAD_DIM))
    return jnp.asarray(blocks, dtype=jnp.bfloat16)


def _const_spec(shape):
    if shape is None:
        return pl.BlockSpec(memory_space=pl.ANY)
    return pl.BlockSpec(shape, lambda s: (0,) * len(shape), pipeline_mode=pl.Buffered(1))


@jax.jit
def kernel(x, c, ada_w, ada_b, norm1_g, w_in, lru_conv_w, lru_conv_b, gate_a_w, gate_a_b, gate_x_w,
           gate_x_b, a_param, short_conv_w, lru_out_g, conv_out_g, w_out, norm2_g, w_mlp1, w_mlp2,
           final_g):
    bsz, seq, d = x.shape
    assert (d, seq % TIME_CHUNK, bsz) == (D_MODEL, 0, SUBLANES)
    assert ada_w.shape[0] == 1, "one layer"

    operands = [
        (c, (bsz, d)),
        (ada_w.reshape(d, N_ADA * d), None),
        (ada_b, (1, N_ADA * d)),
        (norm1_g, (1, d)),
        (w_in.reshape(d, D_IN), None),
        (lru_conv_w.reshape(4, D_LRU), (4, D_LRU)),
        (lru_conv_b, (1, D_LRU)),
        (_gate_weights(gate_a_w, gate_x_w), (D_LRU // MXU_DIM, MXU_DIM, 2 * MXU_DIM)),
        (gate_a_b, (1, D_LRU)),
        (gate_x_b, (1, D_LRU)),
        (a_param, (1, D_LRU)),
        (short_conv_w.reshape(3, D_CONV), (3, D_CONV)),
        (lru_out_g, (1, D_LRU)),
        (conv_out_g, (1, D_CONV)),
        (_head_pool_matrix(), (MXU_DIM, MXU_DIM)),
        (w_out.reshape(D_LRU + D_CONV, d), None),
        (norm2_g, (1, d)),
        (w_mlp1.reshape(d, D_FF), None),
        (w_mlp2.reshape(D_FF, d), None),
        (final_g.reshape(1, d), (1, d)),
    ]
    n_chunks = seq // TIME_CHUNK
    block = (bsz, TIME_CHUNK, d)
    rows = bsz * TIME_CHUNK
    return pl.pallas_call(
        _block_kernel,
        grid=(n_chunks + 1,),
        in_specs=[pl.BlockSpec(block, lambda s: (0, jnp.minimum(s, n_chunks - 1), 0))]
        + [_const_spec(shape) for _, shape in operands],
        out_specs=pl.BlockSpec(block, lambda s: (0, jnp.maximum(s - 1, 0), 0)),
        out_shape=jax.ShapeDtypeStruct(x.shape, x.dtype),
        scratch_shapes=[
            pltpu.VMEM((bsz, SUBLANES, D_LRU), jnp.float32),
            pltpu.VMEM((bsz, SUBLANES, D_CONV), jnp.float32),
            pltpu.VMEM((bsz, D_LRU), jnp.float32),
            pltpu.VMEM((N_SLABS, bsz * SCAN_PITCH, LANES), jnp.float32),
            pltpu.VMEM((N_SLABS, bsz * SCAN_PITCH, LANES), jnp.float32),
            pltpu.VMEM((N_SLABS, bsz * SCAN_PITCH, LANES), jnp.float32),
            pltpu.VMEM((rows, d), jnp.bfloat16),
            pltpu.VMEM(block, jnp.float32),
            pltpu.VMEM((rows, d), jnp.bfloat16),
            pltpu.VMEM((rows, D_FF), jnp.bfloat16),
            pltpu.VMEM((rows, d), jnp.float32),
            pltpu.VMEM((rows, D_LRU), jnp.bfloat16),
            pltpu.VMEM((rows, D_LRU), jnp.bfloat16),
            pltpu.VMEM((rows, D_CONV), jnp.bfloat16),
            pltpu.VMEM((rows, D_LRU + D_CONV), jnp.bfloat16),
            pltpu.VMEM((d, D_IN), jnp.bfloat16),
            pltpu.VMEM((D_LRU + D_CONV, d), jnp.bfloat16),
            pltpu.VMEM((d, D_FF), jnp.bfloat16),
            pltpu.VMEM((D_FF, d), jnp.bfloat16),
            pltpu.VMEM((STAGE_SLOTS, STAGE_ROWS, STAGE_COLS), jnp.float32),
            pltpu.SemaphoreType.DMA((STAGE_SLOTS,)),
            pltpu.VMEM((bsz, N_ADA * d), jnp.float32),
            pltpu.VMEM((N_ADA, bsz, 1, d), jnp.float32),
        ],
        compiler_params=pltpu.CompilerParams(
            dimension_semantics=("arbitrary",), vmem_limit_bytes=VMEM_LIMIT_BYTES),
        name="hybrid_block",
    )(x, *[v for v, _ in operands])
```

```python
import jax
import jax.numpy as jnp
import numpy as np
from jax import lax
from jax.experimental import pallas as pl
from jax.experimental.pallas import tpu as pltpu

D_MODEL = 1024
D_LRU = 512
D_CONV = 512
N_HEADS = 8
HEAD_DIM = 64
HEAD_SHIFT = 6
D_FF = 4096
D_IN = 2 * D_LRU + 3 * D_CONV
N_ADA = 6
C_GATE = 8.0
EPS = 1e-6
LOG2_E = 1.4426950408889634
GELU_C0 = 0.7978845608028654
GELU_C1 = 0.044715

LANES = 128
SUBLANES = 8
MXU_DIM = 256
HEADS_PER_MXU_TILE = MXU_DIM // HEAD_DIM

TIME_CHUNK = 64
FF_CHUNK = 1024
STAGE_ROWS, STAGE_COLS = 256, 512
STAGE_SLOTS = 8
SCAN_PITCH = TIME_CHUNK + SUBLANES
N_SLABS = D_LRU // LANES

VMEM_LIMIT_BYTES = 56 * 1024 * 1024


def _rms_scale(v):
    return lax.rsqrt(jnp.mean(v * v, axis=-1, keepdims=True) + EPS)


def _time_shift(u, prev_tail, j):
    rolled = pltpu.roll(u, j, axis=1)
    rolled_tail = pltpu.roll(prev_tail, j, axis=1)
    t_idx = lax.broadcasted_iota(jnp.int32, (1, SUBLANES, 1), 1)
    head = jnp.where(t_idx < j, rolled_tail, rolled[:, :SUBLANES, :])
    return jnp.concatenate([head, rolled[:, SUBLANES:, :]], axis=1)


def _head_mean(sq_ref, pool):
    halves = [
        jnp.dot(sq_ref[:, k * MXU_DIM:(k + 1) * MXU_DIM], pool, preferred_element_type=jnp.float32)
        for k in range(sq_ref.shape[1] // MXU_DIM)
    ]
    return jnp.concatenate(halves, axis=1)


def _stream_from_hbm(sources, stage, sems):
    n_slots, stage_rows, stage_cols = stage.shape
    chunks = []
    for src, consume in sources:
        n_rows, n_cols = src.shape
        assert n_rows % stage_rows == 0 and n_cols % stage_cols == 0
        for r in range(0, n_rows, stage_rows):
            for c in range(0, n_cols, stage_cols):
                chunks.append((src, consume, r, c))

    def copy(i):
        src, _, r, c = chunks[i]
        slot = i % n_slots
        return pltpu.make_async_copy(src.at[pl.ds(r, stage_rows), pl.ds(c, stage_cols)],
                                     stage.at[slot], sems.at[slot])

    lookahead = n_slots - 1
    for i in range(min(lookahead, len(chunks))):
        copy(i).start()
    for i, (_, consume, r, c) in enumerate(chunks):
        if i + lookahead < len(chunks):
            copy(i + lookahead).start()
        copy(i).wait()
        consume(r, c, stage.at[i % n_slots])


def _block_kernel(x_ref, c_ref, ada_w_hbm, ada_b_ref, n1g_ref, w_in_hbm, lcw_ref, lcb_ref, gaw_ref,
                  gxw_ref, gab_ref, gxb_ref, ap_ref, scw_ref, log_ref, cog_ref, pool_ref, w_out_hbm, n2g_ref,
                  w1_hbm, w2_hbm, fg_ref, o_ref,
                  lx_tail, cv_tail, h_state, a_slab, b_slab, h_slab, hb_ref, x1_ref, h2_ref, hid_ref, mlp_ref,
                  xlb_ref, sql_ref, sqc_ref, yb_ref,
                  w_in_ref, w_out_ref, w1_ref, w2_ref, stage_ref, stage_sems, mod_acc, mod_ref, wg_ref):
    step = pl.program_id(0)
    bsz, tc, d = x_ref.shape
    rows = bsz * tc
    f32, bf16 = jnp.float32, jnp.bfloat16

    def run(mix, out, first):
        shift1, scale1, gate1, shift2, scale2, gate2 = [mod_ref[k] for k in range(N_ADA)]

        def mlp_out_piece(k, k_half=None):
            cols = slice(k * MXU_DIM, (k + 1) * MXU_DIM)
            if k_half is None:
                mlp_ref[:, cols] = jnp.dot(hid_ref[...], w2_ref[:, cols], preferred_element_type=f32)
                return
            feats = slice(k_half * (D_FF // 2), (k_half + 1) * (D_FF // 2))
            part = jnp.dot(hid_ref[:, feats], w2_ref[feats, cols], preferred_element_type=f32)
            if k_half == 0:
                mlp_ref[:, cols] = part
            else:
                mlp_ref[:, cols] += part

        def proj(k, width):
            u = jnp.dot(hb_ref[...], w_in_ref[:, k:k + width], preferred_element_type=f32)
            return u.reshape(bsz, tc, width)

        if out:
            mlp_out_piece(0)
        if not mix:
            for k in range(1, d // MXU_DIM):
                mlp_out_piece(k)
            x2 = x1_ref[...] + gate2 * mlp_ref[...].reshape(bsz, tc, d)
            o_ref[...] = x2 * _rms_scale(x2) * fg_ref[...][None]
            return
        x = x_ref[...]
        h = x * _rms_scale(x) * (n1g_ref[...][None] * (1.0 + scale1)) + shift1
        hb_ref[...] = h.reshape(rows, d).astype(bf16)

        u_lx = proj(0, D_LRU)
        u_c = proj(2 * D_LRU + D_CONV, D_CONV)
        u_v = proj(2 * D_LRU + 2 * D_CONV, D_CONV)
        lcw = lcw_ref[...]
        prev_lx = lx_tail[...]
        xl = u_lx * lcw[3][None, None, :] + lcb_ref[...][None]
        for j in range(1, 4):
            xl = xl + _time_shift(u_lx, prev_lx, j) * lcw[3 - j][None, None, :]
        lx_tail[...] = u_lx[:, tc - SUBLANES:, :]
        xlb_ref[...] = xl.reshape(rows, D_LRU).astype(bf16)
        scw = scw_ref[...]
        cv = u_c * u_v
        prev_cv = cv_tail[...]
        conv = cv * scw[2][None, None, :]
        for j in range(1, 3):
            conv = conv + _time_shift(cv, prev_cv, j) * scw[2 - j][None, None, :]
        cv_tail[...] = cv[:, tc - SUBLANES:, :]

        ap = ap_ref[...]
        softplus = jnp.maximum(ap, 0.0) + jnp.log1p(jnp.exp(-jnp.abs(ap)))
        half_rate = ((-0.5 * C_GATE * LOG2_E) * softplus)[None]
        half_ab = (0.5 * gab_ref[...])[None]
        half_xb = (0.5 * gxb_ref[...])[None]
        first_row = (lax.broadcasted_iota(jnp.int32, (1, SUBLANES, 1), 1) == 0) & (step == 0)

        def lru_inputs(k):
            ch = slice(k * MXU_DIM, (k + 1) * MXU_DIM)
            g = jnp.dot(xlb_ref[:, ch], wg_ref[k], preferred_element_type=f32)
            ga = g[:, :MXU_DIM].reshape(bsz, tc, MXU_DIM)
            gx = g[:, MXU_DIM:].reshape(bsz, tc, MXU_DIM)
            a = jnp.exp2(jnp.tanh(ga + half_ab[:, :, ch]) * half_rate[:, :, ch] + half_rate[:, :, ch])
            i = 0.5 * jnp.tanh(gx + half_xb[:, :, ch]) + 0.5
            v = 1.0 - a * a
            mult = jnp.where(v > 0.0, v * lax.rsqrt(v), 0.0)
            if first:
                mult_head = jnp.where(first_row, 1.0, mult[:, :SUBLANES, :])
                mult = jnp.concatenate([mult_head, mult[:, SUBLANES:, :]], axis=1)
            bx = mult * (i * xl[:, :, ch])
            for b in range(bsz):
                for j in range(MXU_DIM // LANES):
                    s = k * (MXU_DIM // LANES) + j
                    rows_b = pl.ds(b * SCAN_PITCH, tc)
                    a_slab[s, rows_b, :] = a[b, :, j * LANES:(j + 1) * LANES]
                    b_slab[s, rows_b, :] = bx[b, :, j * LANES:(j + 1) * LANES]

        lru_inputs(0)
        u_b = proj(2 * D_LRU, D_CONV)
        if out:
            mlp_out_piece(1, 0)
        lru_inputs(1)
        u_ly = proj(D_LRU, D_LRU)
        if out:
            mlp_out_piece(1, 1)
        y_c =(u_b * conv).reshape(rows, D_CONV)
        sqc_ref[...] = (y_c * y_c).astype(bf16)
        half_ly = 0.5 * u_ly
        gelu_ly = half_ly * jnp.tanh(u_ly * ((GELU_C0 * GELU_C1) * (u_ly * u_ly) + GELU_C0)) + half_ly
        pool = pool_ref[...]
        y_c = y_c * lax.rsqrt(_head_mean(sqc_ref, pool) + EPS) * cog_ref[...]
        yb_ref[:, D_LRU:] = y_c.astype(bf16)
        mixed_c = jnp.dot(yb_ref[:, D_LRU:], w_out_ref[D_LRU:, :], preferred_element_type=f32)

        if out:
            mlp_out_piece(2)

        def scan_step(t, carry):
            new = []
            for s in range(N_SLABS):
                idx = pl.ds(t, bsz, stride=SCAN_PITCH)
                h_t = a_slab[s, idx, :] * carry[s] + b_slab[s, idx, :]
                h_slab[s, idx, :] = h_t
                new.append(h_t)
            return tuple(new)

        h0 = tuple(h_state[:, s * LANES:(s + 1) * LANES] for s in range(N_SLABS))
        h_last = lax.fori_loop(0, tc, scan_step, h0, unroll=True)
        for s in range(N_SLABS):
            h_state[:, s * LANES:(s + 1) * LANES] = h_last[s]
        hl = jnp.stack([
            jnp.concatenate([h_slab[s, pl.ds(b * SCAN_PITCH, tc), :] for s in range(N_SLABS)], axis=1)
            for b in range(bsz)], axis=0)
        y_l = (gelu_ly * hl).reshape(rows, D_LRU)
        sql_ref[...] = (y_l * y_l).astype(bf16)
        y_l = y_l * lax.rsqrt(_head_mean(sql_ref, pool) + EPS) * log_ref[...]
        yb_ref[:, :D_LRU] = y_l.astype(bf16)

        mixed = mixed_c + jnp.dot(yb_ref[:, :D_LRU], w_out_ref[:D_LRU, :], preferred_element_type=f32)
        if out:
            mlp_out_piece(3)
            x1_prev = x1_ref[...]
        x1 = x_ref[...] + gate1 * mixed.reshape(bsz, tc, d)
        x1_ref[...] = x1
        h2 = x1 * _rms_scale(x1) * (n2g_ref[...][None] * (1.0 + scale2)) + shift2
        h2_ref[...] = h2.reshape(rows, d).astype(bf16)

        for k in range(D_FF // FF_CHUNK):
            cols = slice(k * FF_CHUNK, (k + 1) * FF_CHUNK)
            hk = jnp.dot(h2_ref[...], w1_ref[:, cols], preferred_element_type=f32)
            hid_ref[:, cols] = jnp.square(jnp.maximum(hk, 0.0)).astype(bf16)
            if k == 0 and out:
                x2 = x1_prev + gate2 * mlp_ref[...].reshape(bsz, tc, d)
                o_ref[...] = x2 * _rms_scale(x2) * fg_ref[...][None]

    @pl.when(step == 0)
    def _():
        lx_tail[...] = jnp.zeros_like(lx_tail)
        cv_tail[...] = jnp.zeros_like(cv_tail)
        h_state[...] = jnp.zeros_like(h_state)
        x1_ref[...] = jnp.zeros_like(x1_ref)
        hid_ref[...] = jnp.zeros_like(hid_ref)

        sc = jax.nn.silu(c_ref[...]).astype(bf16)
        mod_acc[...] = jnp.broadcast_to(ada_b_ref[...], mod_acc.shape)

        def ada_chunk(r, c, chunk):
            part = jnp.dot(sc[:, r:r + chunk.shape[0]], chunk[...].astype(bf16),
                           preferred_element_type=f32)
            mod_acc[:, c:c + chunk.shape[1]] += part

        def cast_into(dst):
            def consume(r, c, chunk):
                dst[r:r + chunk.shape[0], c:c + chunk.shape[1]] = chunk[...].astype(dst.dtype)
            return consume

        _stream_from_hbm([(ada_w_hbm, ada_chunk), (w_in_hbm, cast_into(w_in_ref)),
                          (w_out_hbm, cast_into(w_out_ref)), (w1_hbm, cast_into(w1_ref)),
                          (w2_hbm, cast_into(w2_ref))], stage_ref, stage_sems)
        for k in range(N_ADA):
            for b in range(bsz):
                mod_ref[k, b] = mod_acc[b:b + 1, k * d:(k + 1) * d]

        col = lax.broadcasted_iota(jnp.int32, (HEAD_DIM, MXU_DIM), 1)
        row = lax.broadcasted_iota(jnp.int32, (HEAD_DIM, MXU_DIM), 0)
        expand = jnp.where((col & (HEAD_DIM - 1)) == row, 1.0, 0.0).astype(bf16)
        head_of_row = lax.broadcasted_iota(jnp.int32, (MXU_DIM, MXU_DIM), 0) >> HEAD_SHIFT
        head_of_col = lax.broadcasted_iota(jnp.int32, (MXU_DIM, MXU_DIM), 1) >> HEAD_SHIFT
        for t in range(D_LRU // MXU_DIM):
            for g, w_ref in enumerate((gaw_ref, gxw_ref)):
                w4 = (0.5 * w_ref[t * MXU_DIM:(t + 1) * MXU_DIM, :]).astype(bf16)
                tiled = jnp.dot(w4, expand, preferred_element_type=f32)
                wg_ref[t, :, g * MXU_DIM:(g + 1) * MXU_DIM] = jnp.where(
                    head_of_row == head_of_col, tiled, 0.0).astype(bf16)

    run(mix=True, out=True, first=True)


def _head_pool_matrix():
    blocks = np.kron(np.eye(HEADS_PER_MXU_TILE), np.full((HEAD_DIM, HEAD_DIM), 1.0 / HEAD_DIM))
    return jnp.asarray(blocks, dtype=jnp.bfloat16)


def _const_spec(shape):
    if shape is None:
        return pl.BlockSpec(memory_space=pl.ANY)
    return pl.BlockSpec(shape, lambda s: (0,) * len(shape), pipeline_mode=pl.Buffered(1))


@jax.jit
def kernel(x, c, ada_w, ada_b, norm1_g, w_in, lru_conv_w, lru_conv_b, gate_a_w, gate_a_b, gate_x_w,
           gate_x_b, a_param, short_conv_w, lru_out_g, conv_out_g, w_out, norm2_g, w_mlp1, w_mlp2,
           final_g):
    bsz, seq, d = x.shape
    assert (d, seq % TIME_CHUNK, bsz) == (D_MODEL, 0, SUBLANES)
    assert ada_w.shape[0] == 1, "one layer"

    operands = [
        (c, (bsz, d)),
        (ada_w.reshape(d, N_ADA * d), None),
        (ada_b, (1, N_ADA * d)),
        (norm1_g, (1, d)),
        (w_in.reshape(d, D_IN), None),
        (lru_conv_w.reshape(4, D_LRU), (4, D_LRU)),
        (lru_conv_b, (1, D_LRU)),
        (gate_a_w.reshape(D_LRU, HEAD_DIM), (D_LRU, HEAD_DIM)),
        (gate_x_w.reshape(D_LRU, HEAD_DIM), (D_LRU, HEAD_DIM)),
        (gate_a_b, (1, D_LRU)),
        (gate_x_b, (1, D_LRU)),
        (a_param, (1, D_LRU)),
        (short_conv_w.reshape(3, D_CONV), (3, D_CONV)),
        (lru_out_g, (1, D_LRU)),
        (conv_out_g, (1, D_CONV)),
        (_head_pool_matrix(), (MXU_DIM, MXU_DIM)),
        (w_out.reshape(D_LRU + D_CONV, d), None),
        (norm2_g, (1, d)),
        (w_mlp1.reshape(d, D_FF), None),
        (w_mlp2.reshape(D_FF, d), None),
        (final_g.reshape(1, d), (1, d)),
    ]
    n_chunks = seq // TIME_CHUNK
    block = (bsz, TIME_CHUNK, d)
    rows = bsz * TIME_CHUNK
    return pl.pallas_call(
        _block_kernel,
        grid=(n_chunks + 1,),
        in_specs=[pl.BlockSpec(block, lambda s: (0, jnp.minimum(s, n_chunks - 1), 0))]
        + [_const_spec(shape) for _, shape in operands],
        out_specs=pl.BlockSpec(block, lambda s: (0, jnp.maximum(s - 1, 0), 0)),
        out_shape=jax.ShapeDtypeStruct(x.shape, x.dtype),
        scratch_shapes=[
            pltpu.VMEM((bsz, SUBLANES, D_LRU), jnp.float32),
            pltpu.VMEM((bsz, SUBLANES, D_CONV), jnp.float32),
            pltpu.VMEM((bsz, D_LRU), jnp.float32),
            pltpu.VMEM((N_SLABS, bsz * SCAN_PITCH, LANES), jnp.float32),
            pltpu.VMEM((N_SLABS, bsz * SCAN_PITCH, LANES), jnp.float32),
            pltpu.VMEM((N_SLABS, bsz * SCAN_PITCH, LANES), jnp.float32),
            pltpu.VMEM((rows, d), jnp.bfloat16),
            pltpu.VMEM(block, jnp.float32),
            pltpu.VMEM((rows, d), jnp.bfloat16),
            pltpu.VMEM((rows, D_FF), jnp.bfloat16),
            pltpu.VMEM((rows, d), jnp.float32),
            pltpu.VMEM((rows, D_LRU), jnp.bfloat16),
            pltpu.VMEM((rows, D_LRU), jnp.bfloat16),
            pltpu.VMEM((rows, D_CONV), jnp.bfloat16),
            pltpu.VMEM((rows, D_LRU + D_CONV), jnp.bfloat16),
            pltpu.VMEM((d, D_IN), jnp.bfloat16),
            pltpu.VMEM((D_LRU + D_CONV, d), jnp.bfloat16),
            pltpu.VMEM((d, D_FF), jnp.bfloat16),
            pltpu.VMEM((D_FF, d), jnp.bfloat16),
            pltpu.VMEM((STAGE_SLOTS, STAGE_ROWS, STAGE_COLS), jnp.float32),
            pltpu.SemaphoreType.DMA((STAGE_SLOTS,)),
            pltpu.VMEM((bsz, N_ADA * d), jnp.float32),
            pltpu.VMEM((N_ADA, bsz, 1, d), jnp.float32),
            pltpu.VMEM((D_LRU // MXU_DIM, MXU_DIM, 2 * MXU_DIM), jnp.bfloat16),
        ],
        compiler_params=pltpu.CompilerParams(
            dimension_semantics=("arbitrary",), vmem_limit_bytes=VMEM_LIMIT_BYTES),
        name="hybrid_block",
    )(x, *[v for v, _ in operands])
```

```python
import jax
import jax.numpy as jnp
import numpy as np
from jax import lax
from jax.experimental import pallas as pl
from jax.experimental.pallas import tpu as pltpu

D_MODEL = 1024
D_LRU = 512
D_CONV = 512
N_HEADS = 8
HEAD_DIM = 64
HEAD_SHIFT = 6
D_FF = 4096
D_IN = 2 * D_LRU + 3 * D_CONV
N_ADA = 6
C_GATE = 8.0
EPS = 1e-6
LOG2_E = 1.4426950408889634
GELU_C0 = 0.7978845608028654
GELU_C1 = 0.044715

LANES = 128
SUBLANES = 8
MXU_DIM = 256
HEADS_PER_MXU_TILE = MXU_DIM // HEAD_DIM

TIME_CHUNK = 64
FF_CHUNK = 1024
STAGE_ROWS, STAGE_COLS = 256, 512
STAGE_SLOTS = 8
SCAN_PITCH = TIME_CHUNK + SUBLANES
N_SLABS = D_LRU // LANES

VMEM_LIMIT_BYTES = 56 * 1024 * 1024


def _rms_scale(v):
    return lax.rsqrt(jnp.mean(v * v, axis=-1, keepdims=True) + EPS)


def _time_shift(u, prev_tail, j):
    rolled = pltpu.roll(u, j, axis=1)
    rolled_tail = pltpu.roll(prev_tail, j, axis=1)
    t_idx = lax.broadcasted_iota(jnp.int32, (1, SUBLANES, 1), 1)
    head = jnp.where(t_idx < j, rolled_tail, rolled[:, :SUBLANES, :])
    return jnp.concatenate([head, rolled[:, SUBLANES:, :]], axis=1)


def _head_mean(sq_ref, pool):
    halves = [
        jnp.dot(sq_ref[:, k * MXU_DIM:(k + 1) * MXU_DIM], pool, preferred_element_type=jnp.float32)
        for k in range(sq_ref.shape[1] // MXU_DIM)
    ]
    return jnp.concatenate(halves, axis=1)


def _stream_from_hbm(sources, stage, sems):
    n_slots, stage_rows, stage_cols = stage.shape
    chunks = []
    for src, consume in sources:
        n_rows, n_cols = src.shape
        assert n_rows % stage_rows == 0 and n_cols % stage_cols == 0
        for r in range(0, n_rows, stage_rows):
            for c in range(0, n_cols, stage_cols):
                chunks.append((src, consume, r, c))

    def copy(i):
        src, _, r, c = chunks[i]
        slot = i % n_slots
        return pltpu.make_async_copy(src.at[pl.ds(r, stage_rows), pl.ds(c, stage_cols)],
                                     stage.at[slot], sems.at[slot])

    lookahead = n_slots - 1
    for i in range(min(lookahead, len(chunks))):
        copy(i).start()
    for i, (_, consume, r, c) in enumerate(chunks):
        if i + lookahead < len(chunks):
            copy(i + lookahead).start()
        copy(i).wait()
        consume(r, c, stage.at[i % n_slots])


def _block_kernel(x_ref, c_ref, ada_w_hbm, ada_b_ref, n1g_ref, w_in_hbm, lcw_ref, lcb_ref, gaw_ref,
                  gxw_ref, gab_ref, gxb_ref, ap_ref, scw_ref, log_ref, cog_ref, pool_ref, w_out_hbm, n2g_ref,
                  w1_hbm, w2_hbm, fg_ref, o_ref,
                  lx_tail, cv_tail, h_state, a_slab, b_slab, h_slab, hb_ref, x1_ref, h2_ref, hid_ref, mlp_ref,
                  xlb_ref, sql_ref, sqc_ref, yb_ref,
                  w_in_ref, w_out_ref, w1_ref, w2_ref, stage_ref, stage_sems, mod_acc, mod_ref, wg_ref):
    step = pl.program_id(0)
    bsz, tc, d = x_ref.shape
    rows = bsz * tc
    f32, bf16 = jnp.float32, jnp.bfloat16

    def run(mix, out, first):
        shift1, scale1, gate1, shift2, scale2, gate2 = [mod_ref[k] for k in range(N_ADA)]

        def mlp_out_piece(k, k_half=None):
            cols = slice(k * MXU_DIM, (k + 1) * MXU_DIM)
            if k_half is None:
                mlp_ref[:, cols] = jnp.dot(hid_ref[...], w2_ref[:, cols], preferred_element_type=f32)
                return
            feats = slice(k_half * (D_FF // 2), (k_half + 1) * (D_FF // 2))
            part = jnp.dot(hid_ref[:, feats], w2_ref[feats, cols], preferred_element_type=f32)
            if k_half == 0:
                mlp_ref[:, cols] = part
            else:
                mlp_ref[:, cols] += part

        def proj(k, width):
            u = jnp.dot(hb_ref[...], w_in_ref[:, k:k + width], preferred_element_type=f32)
            return u.reshape(bsz, tc, width)

        if out:
            mlp_out_piece(0)
        if not mix:
            for k in range(1, d // MXU_DIM):
                mlp_out_piece(k)
            x2 = x1_ref[...] + gate2 * mlp_ref[...].reshape(bsz, tc, d)
            o_ref[...] = x2 * _rms_scale(x2) * fg_ref[...][None]
            return
        x = x_ref[...]
        h = x * _rms_scale(x) * (n1g_ref[...][None] * (1.0 + scale1)) + shift1
        hb_ref[...] = h.reshape(rows, d).astype(bf16)

        u_lx = proj(0, D_LRU)
        u_c = proj(2 * D_LRU + D_CONV, D_CONV)
        u_v = proj(2 * D_LRU + 2 * D_CONV, D_CONV)
        lcw = lcw_ref[...]
        prev_lx = lx_tail[...]
        xl = u_lx * lcw[3][None, None, :] + lcb_ref[...][None]
        for j in range(1, 4):
            xl = xl + _time_shift(u_lx, prev_lx, j) * lcw[3 - j][None, None, :]
        lx_tail[...] = u_lx[:, tc - SUBLANES:, :]
        xlb_ref[...] = xl.reshape(rows, D_LRU).astype(bf16)
        scw = scw_ref[...]
        cv = u_c * u_v
        prev_cv = cv_tail[...]
        conv = cv * scw[2][None, None, :]
        for j in range(1, 3):
            conv = conv + _time_shift(cv, prev_cv, j) * scw[2 - j][None, None, :]
        cv_tail[...] = cv[:, tc - SUBLANES:, :]

        ap = ap_ref[...]
        softplus = jnp.maximum(ap, 0.0) + jnp.log1p(jnp.exp(-jnp.abs(ap)))
        half_rate = ((-0.5 * C_GATE * LOG2_E) * softplus)[None]
        half_ab = (0.5 * gab_ref[...])[None]
        half_xb = (0.5 * gxb_ref[...])[None]
        first_row = (lax.broadcasted_iota(jnp.int32, (1, SUBLANES, 1), 1) == 0) & (step == 0)

        def lru_inputs(k):
            ch = slice(k * MXU_DIM, (k + 1) * MXU_DIM)
            g = jnp.dot(xlb_ref[:, ch], wg_ref[k], preferred_element_type=f32)
            ga = g[:, :MXU_DIM].reshape(bsz, tc, MXU_DIM)
            gx = g[:, MXU_DIM:].reshape(bsz, tc, MXU_DIM)
            a = jnp.exp2(jnp.tanh(ga + half_ab[:, :, ch]) * half_rate[:, :, ch] + half_rate[:, :, ch])
            i = 0.5 * jnp.tanh(gx + half_xb[:, :, ch]) + 0.5
            v = 1.0 - a * a
            mult = jnp.where(v > 0.0, v * lax.rsqrt(v), 0.0)
            if first:
                mult_head = jnp.where(first_row, 1.0, mult[:, :SUBLANES, :])
                mult = jnp.concatenate([mult_head, mult[:, SUBLANES:, :]], axis=1)
            bx = mult * (i * xl[:, :, ch])
            for b in range(bsz):
                for j in range(MXU_DIM // LANES):
                    s = k * (MXU_DIM // LANES) + j
                    rows_b = pl.ds(b * SCAN_PITCH, tc)
                    a_slab[s, rows_b, :] = a[b, :, j * LANES:(j + 1) * LANES]
                    b_slab[s, rows_b, :] = bx[b, :, j * LANES:(j + 1) * LANES]

        lru_inputs(0)
        u_b = proj(2 * D_LRU, D_CONV)
        if out:
            mlp_out_piece(1, 0)
        lru_inputs(1)
        u_ly = proj(D_LRU, D_LRU)
        if out:
            mlp_out_piece(1, 1)
        y_c =(u_b * conv).reshape(rows, D_CONV)
        sqc_ref[...] = (y_c * y_c).astype(bf16)
        half_ly = 0.5 * u_ly
        gelu_ly = half_ly * jnp.tanh(u_ly * ((GELU_C0 * GELU_C1) * (u_ly * u_ly) + GELU_C0)) + half_ly
        pool = pool_ref[...]
        y_c = y_c * lax.rsqrt(_head_mean(sqc_ref, pool) + EPS) * cog_ref[...]
        yb_ref[:, D_LRU:] = y_c.astype(bf16)
        mixed_c = jnp.dot(yb_ref[:, D_LRU:], w_out_ref[D_LRU:, :], preferred_element_type=f32)

        if out:
            mlp_out_piece(2)

        def scan_step(t, carry):
            new = []
            for s in range(N_SLABS):
                idx = pl.ds(t, bsz, stride=SCAN_PITCH)
                h_t = a_slab[s, idx, :] * carry[s] + b_slab[s, idx, :]
                h_slab[s, idx, :] = h_t
                new.append(h_t)
            return tuple(new)

        h0 = tuple(h_state[:, s * LANES:(s + 1) * LANES] for s in range(N_SLABS))
        h_last = lax.fori_loop(0, tc, scan_step, h0, unroll=True)
        for s in range(N_SLABS):
            h_state[:, s * LANES:(s + 1) * LANES] = h_last[s]
        hl = jnp.stack([
            jnp.concatenate([h_slab[s, pl.ds(b * SCAN_PITCH, tc), :] for s in range(N_SLABS)], axis=1)
            for b in range(bsz)], axis=0)
        y_l = (gelu_ly * hl).reshape(rows, D_LRU)
        sql_ref[...] = (y_l * y_l).astype(bf16)
        y_l = y_l * lax.rsqrt(_head_mean(sql_ref, pool) + EPS) * log_ref[...]
        yb_ref[:, :D_LRU] = y_l.astype(bf16)

        mixed = mixed_c + jnp.dot(yb_ref[:, :D_LRU], w_out_ref[:D_LRU, :], preferred_element_type=f32)
        if out:
            mlp_out_piece(3)
            x1_prev = x1_ref[...]
        x1 = x_ref[...] + gate1 * mixed.reshape(bsz, tc, d)
        x1_ref[...] = x1
        h2 = x1 * _rms_scale(x1) * (n2g_ref[...][None] * (1.0 + scale2)) + shift2
        h2_ref[...] = h2.reshape(rows, d).astype(bf16)

        for k in range(D_FF // FF_CHUNK):
            cols = slice(k * FF_CHUNK, (k + 1) * FF_CHUNK)
            hk = jnp.dot(h2_ref[...], w1_ref[:, cols], preferred_element_type=f32)
            hid_ref[:, cols] = jnp.square(jnp.maximum(hk, 0.0)).astype(bf16)
            if k == 0 and out:
                x2 = x1_prev + gate2 * mlp_ref[...].reshape(bsz, tc, d)
                o_ref[...] = x2 * _rms_scale(x2) * fg_ref[...][None]

    @pl.when(step == 0)
    def _():
        lx_tail[...] = jnp.zeros_like(lx_tail)
        cv_tail[...] = jnp.zeros_like(cv_tail)
        h_state[...] = jnp.zeros_like(h_state)
        x1_ref[...] = jnp.zeros_like(x1_ref)
        hid_ref[...] = jnp.zeros_like(hid_ref)

        sc = jax.nn.silu(c_ref[...]).astype(bf16)
        mod_acc[...] = jnp.broadcast_to(ada_b_ref[...], mod_acc.shape)

        def ada_chunk(r, c, chunk):
            part = jnp.dot(sc[:, r:r + chunk.shape[0]], chunk[...].astype(bf16),
                           preferred_element_type=f32)
            mod_acc[:, c:c + chunk.shape[1]] += part

        def cast_into(dst):
            def consume(r, c, chunk):
                dst[r:r + chunk.shape[0], c:c + chunk.shape[1]] = chunk[...].astype(dst.dtype)
            return consume

        _stream_from_hbm([(ada_w_hbm, ada_chunk), (w_in_hbm, cast_into(w_in_ref)),
                          (w_out_hbm, cast_into(w_out_ref)), (w1_hbm, cast_into(w1_ref)),
                          (w2_hbm, cast_into(w2_ref))], stage_ref, stage_sems)
        for k in range(N_ADA):
            for b in range(bsz):
                mod_ref[k, b] = mod_acc[b:b + 1, k * d:(k + 1) * d]

        col = lax.broadcasted_iota(jnp.int32, (HEAD_DIM, MXU_DIM), 1)
        row = lax.broadcasted_iota(jnp.int32, (HEAD_DIM, MXU_DIM), 0)
        expand = jnp.where((col & (HEAD_DIM - 1)) == row, 1.0, 0.0).astype(bf16)
        head_of_row = lax.broadcasted_iota(jnp.int32, (MXU_DIM, MXU_DIM), 0) >> HEAD_SHIFT
        head_of_col = lax.broadcasted_iota(jnp.int32, (MXU_DIM, MXU_DIM), 1) >> HEAD_SHIFT
        for t in range(D_LRU // MXU_DIM):
            for g, w_ref in enumerate((gaw_ref, gxw_ref)):
                w4 = (0.5 * w_ref[t * MXU_DIM:(t + 1) * MXU_DIM, :]).astype(bf16)
                tiled = jnp.dot(w4, expand, preferred_element_type=f32)
                wg_ref[t, :, g * MXU_DIM:(g + 1) * MXU_DIM] = jnp.where(
                    head_of_row == head_of_col, tiled, 0.0).astype(bf16)

    last_step = pl.num_programs(0) - 1

    @pl.when(step < last_step)
    def _():
        run(mix=True, out=True, first=True)

    @pl.when(step == last_step)
    def _():
        run(mix=False, out=True, first=False)


def _head_pool_matrix():
    blocks = np.kron(np.eye(HEADS_PER_MXU_TILE), np.full((HEAD_DIM, HEAD_DIM), 1.0 / HEAD_DIM))
    return jnp.asarray(blocks, dtype=jnp.bfloat16)


def _const_spec(shape):
    if shape is None:
        return pl.BlockSpec(memory_space=pl.ANY)
    return pl.BlockSpec(shape, lambda s: (0,) * len(shape), pipeline_mode=pl.Buffered(1))


@jax.jit
def kernel(x, c, ada_w, ada_b, norm1_g, w_in, lru_conv_w, lru_conv_b, gate_a_w, gate_a_b, gate_x_w,
           gate_x_b, a_param, short_conv_w, lru_out_g, conv_out_g, w_out, norm2_g, w_mlp1, w_mlp2,
           final_g):
    bsz, seq, d = x.shape
    assert (d, seq % TIME_CHUNK, bsz) == (D_MODEL, 0, SUBLANES)
    assert ada_w.shape[0] == 1, "one layer"

    operands = [
        (c, (bsz, d)),
        (ada_w.reshape(d, N_ADA * d), None),
        (ada_b, (1, N_ADA * d)),
        (norm1_g, (1, d)),
        (w_in.reshape(d, D_IN), None),
        (lru_conv_w.reshape(4, D_LRU), (4, D_LRU)),
        (lru_conv_b, (1, D_LRU)),
        (gate_a_w.reshape(D_LRU, HEAD_DIM), (D_LRU, HEAD_DIM)),
        (gate_x_w.reshape(D_LRU, HEAD_DIM), (D_LRU, HEAD_DIM)),
        (gate_a_b, (1, D_LRU)),
        (gate_x_b, (1, D_LRU)),
        (a_param, (1, D_LRU)),
        (short_conv_w.reshape(3, D_CONV), (3, D_CONV)),
        (lru_out_g, (1, D_LRU)),
        (conv_out_g, (1, D_CONV)),
        (_head_pool_matrix(), (MXU_DIM, MXU_DIM)),
        (w_out.reshape(D_LRU + D_CONV, d), None),
        (norm2_g, (1, d)),
        (w_mlp1.reshape(d, D_FF), None),
        (w_mlp2.reshape(D_FF, d), None),
        (final_g.reshape(1, d), (1, d)),
    ]
    n_chunks = seq // TIME_CHUNK
    block = (bsz, TIME_CHUNK, d)
    rows = bsz * TIME_CHUNK
    return pl.pallas_call(
        _block_kernel,
        grid=(n_chunks + 1,),
        in_specs=[pl.BlockSpec(block, lambda s: (0, jnp.minimum(s, n_chunks - 1), 0))]
        + [_const_spec(shape) for _, shape in operands],
        out_specs=pl.BlockSpec(block, lambda s: (0, jnp.maximum(s - 1, 0), 0)),
        out_shape=jax.ShapeDtypeStruct(x.shape, x.dtype),
        scratch_shapes=[
            pltpu.VMEM((bsz, SUBLANES, D_LRU), jnp.float32),
            pltpu.VMEM((bsz, SUBLANES, D_CONV), jnp.float32),
            pltpu.VMEM((bsz, D_LRU), jnp.float32),
            pltpu.VMEM((N_SLABS, bsz * SCAN_PITCH, LANES), jnp.float32),
            pltpu.VMEM((N_SLABS, bsz * SCAN_PITCH, LANES), jnp.float32),
            pltpu.VMEM((N_SLABS, bsz * SCAN_PITCH, LANES), jnp.float32),
            pltpu.VMEM((rows, d), jnp.bfloat16),
            pltpu.VMEM(block, jnp.float32),
            pltpu.VMEM((rows, d), jnp.bfloat16),
            pltpu.VMEM((rows, D_FF), jnp.bfloat16),
            pltpu.VMEM((rows, d), jnp.float32),
            pltpu.VMEM((rows, D_LRU), jnp.bfloat16),
            pltpu.VMEM((rows, D_LRU), jnp.bfloat16),
            pltpu.VMEM((rows, D_CONV), jnp.bfloat16),
            pltpu.VMEM((rows, D_LRU + D_CONV), jnp.bfloat16),
            pltpu.VMEM((d, D_IN), jnp.bfloat16),
            pltpu.VMEM((D_LRU + D_CONV, d), jnp.bfloat16),
            pltpu.VMEM((d, D_FF), jnp.bfloat16),
            pltpu.VMEM((D_FF, d), jnp.bfloat16),
            pltpu.VMEM((STAGE_SLOTS, STAGE_ROWS, STAGE_COLS), jnp.float32),
            pltpu.SemaphoreType.DMA((STAGE_SLOTS,)),
            pltpu.VMEM((bsz, N_ADA * d), jnp.float32),
            pltpu.VMEM((N_ADA, bsz, 1, d), jnp.float32),
            pltpu.VMEM((D_LRU // MXU_DIM, MXU_DIM, 2 * MXU_DIM), jnp.bfloat16),
        ],
        compiler_params=pltpu.CompilerParams(
            dimension_semantics=("arbitrary",), vmem_limit_bytes=VMEM_LIMIT_BYTES),
        name="hybrid_block",
    )(x, *[v for v, _ in operands])
```

```python
import jax
import jax.numpy as jnp
import numpy as np
from jax import lax
from jax.experimental import pallas as pl
from jax.experimental.pallas import tpu as pltpu

D_MODEL = 1024
D_LRU = 512
D_CONV = 512
HEAD_DIM = 64
HEAD_SHIFT = 6
D_FF = 4096
D_IN = 2 * D_LRU + 3 * D_CONV
N_ADA = 6
C_GATE = 8.0
EPS = 1e-6
LOG2_E = 1.4426950408889634
GELU_C0 = 0.7978845608028654
GELU_C1 = 0.044715

LANES = 128
SUBLANES = 8
MXU_DIM = 256
HEADS_PER_MXU_TILE = MXU_DIM // HEAD_DIM

TIME_CHUNK = 64
FF_CHUNK = 1024
STAGE_ROWS, STAGE_COLS = 256, 512
STAGE_SLOTS = 8
SCAN_PITCH = TIME_CHUNK + SUBLANES
N_SLABS = D_LRU // LANES


V7X_VMEM_BYTES = 64 * 1024 * 1024
COMPILER_TEMP_BYTES = 4 * 1024 * 1024


def _vmem_limit_bytes(bsz, d):
    rows, f32, bf16 = bsz * TIME_CHUNK, 4, 2
    weights = (d * D_IN + (D_LRU + D_CONV) * d + 2 * d * D_FF + 2 * MXU_DIM * 2 * MXU_DIM) * bf16
    windows = 2 * 2 * rows * d * f32
    staging = STAGE_SLOTS * STAGE_ROWS * STAGE_COLS * f32
    slabs = 3 * N_SLABS * bsz * SCAN_PITCH * LANES * f32
    per_chunk_f32 = 2 * rows * d * f32
    per_chunk_bf16 = rows * (2 * d + D_FF + 2 * D_LRU + D_CONV + D_LRU + D_CONV) * bf16
    small = 2 * bsz * N_ADA * d * f32 + 2 * D_LRU * LANES * f32
    total = (weights + windows + staging + slabs + per_chunk_f32 + per_chunk_bf16 + small
             + COMPILER_TEMP_BYTES)
    assert total <= V7X_VMEM_BYTES, total
    return total


def _rms_scale(v):
    return lax.rsqrt(jnp.mean(v * v, axis=-1, keepdims=True) + EPS)


def _time_shift(u, prev_tail, j):
    rolled = pltpu.roll(u, j, axis=1)
    rolled_tail = pltpu.roll(prev_tail, j, axis=1)
    t_idx = lax.broadcasted_iota(jnp.int32, (1, SUBLANES, 1), 1)
    head = jnp.where(t_idx < j, rolled_tail, rolled[:, :SUBLANES, :])
    return jnp.concatenate([head, rolled[:, SUBLANES:, :]], axis=1)


def _head_mean(sq_ref, pool):
    halves = [
        jnp.dot(sq_ref[:, k * MXU_DIM:(k + 1) * MXU_DIM], pool, preferred_element_type=jnp.float32)
        for k in range(sq_ref.shape[1] // MXU_DIM)
    ]
    return jnp.concatenate(halves, axis=1)


def _stream_from_hbm(sources, stage, sems):
    n_slots, stage_rows, stage_cols = stage.shape
    chunks = []
    for src, consume in sources:
        n_rows, n_cols = src.shape
        assert n_rows % stage_rows == 0 and n_cols % stage_cols == 0
        for r in range(0, n_rows, stage_rows):
            for c in range(0, n_cols, stage_cols):
                chunks.append((src, consume, r, c))

    def copy(i):
        src, _, r, c = chunks[i]
        slot = i % n_slots
        return pltpu.make_async_copy(src.at[pl.ds(r, stage_rows), pl.ds(c, stage_cols)],
                                     stage.at[slot], sems.at[slot])

    lookahead = n_slots - 1
    for i in range(min(lookahead, len(chunks))):
        copy(i).start()
    for i, (_, consume, r, c) in enumerate(chunks):
        if i + lookahead < len(chunks):
            copy(i + lookahead).start()
        copy(i).wait()
        consume(r, c, stage.at[i % n_slots])


def _block_kernel(x_ref, c_ref, ada_w_hbm, ada_b_ref, n1g_ref, w_in_hbm, lcw_ref, lcb_ref, gaw_ref,
                  gxw_ref, gab_ref, gxb_ref, ap_ref, scw_ref, log_ref, cog_ref, pool_ref, w_out_hbm, n2g_ref,
                  w1_hbm, w2_hbm, fg_ref, o_ref,
                  lx_tail, cv_tail, h_state, a_slab, b_slab, h_slab, hb_ref, x1_ref, h2_ref, hid_ref, mlp_ref,
                  xlb_ref, sql_ref, sqc_ref, yb_ref,
                  w_in_ref, w_out_ref, w1_ref, w2_ref, stage_ref, stage_sems, mod_acc, mod_ref, wg_ref):
    step = pl.program_id(0)
    bsz, tc, d = x_ref.shape
    rows = bsz * tc
    f32, bf16 = jnp.float32, jnp.bfloat16

    def run(mix):
        shift1, scale1, gate1, shift2, scale2, gate2 = [mod_ref[k] for k in range(N_ADA)]

        def mlp_out_piece(k, k_half=None):
            cols = slice(k * MXU_DIM, (k + 1) * MXU_DIM)
            if k_half is None:
                mlp_ref[:, cols] = jnp.dot(hid_ref[...], w2_ref[:, cols], preferred_element_type=f32)
                return
            feats = slice(k_half * (D_FF // 2), (k_half + 1) * (D_FF // 2))
            part = jnp.dot(hid_ref[:, feats], w2_ref[feats, cols], preferred_element_type=f32)
            if k_half == 0:
                mlp_ref[:, cols] = part
            else:
                mlp_ref[:, cols] += part

        def proj(k, width):
            u = jnp.dot(hb_ref[...], w_in_ref[:, k:k + width], preferred_element_type=f32)
            return u.reshape(bsz, tc, width)

        mlp_out_piece(0)
        if not mix:
            for k in range(1, d // MXU_DIM):
                mlp_out_piece(k)
            x2 = x1_ref[...] + gate2 * mlp_ref[...].reshape(bsz, tc, d)
            o_ref[...] = x2 * _rms_scale(x2) * fg_ref[...][None]
            return
        x = x_ref[...]
        h = x * _rms_scale(x) * (n1g_ref[...][None] * (1.0 + scale1)) + shift1
        hb_ref[...] = h.reshape(rows, d).astype(bf16)

        u_lx = proj(0, D_LRU)
        u_c = proj(2 * D_LRU + D_CONV, D_CONV)
        u_v = proj(2 * D_LRU + 2 * D_CONV, D_CONV)
        lcw = lcw_ref[...]
        prev_lx = lx_tail[...]
        xl = u_lx * lcw[3][None, None, :] + lcb_ref[...][None]
        for j in range(1, 4):
            xl = xl + _time_shift(u_lx, prev_lx, j) * lcw[3 - j][None, None, :]
        lx_tail[...] = u_lx[:, tc - SUBLANES:, :]
        xlb_ref[...] = xl.reshape(rows, D_LRU).astype(bf16)
        scw = scw_ref[...]
        cv = u_c * u_v
        prev_cv = cv_tail[...]
        conv = cv * scw[2][None, None, :]
        for j in range(1, 3):
            conv = conv + _time_shift(cv, prev_cv, j) * scw[2 - j][None, None, :]
        cv_tail[...] = cv[:, tc - SUBLANES:, :]

        ap = ap_ref[...]
        softplus = jnp.maximum(ap, 0.0) + jnp.log1p(jnp.exp(-jnp.abs(ap)))
        half_rate = ((-0.5 * C_GATE * LOG2_E) * softplus)[None]
        half_ab = (0.5 * gab_ref[...])[None]
        half_xb = (0.5 * gxb_ref[...])[None]
        first_row = (lax.broadcasted_iota(jnp.int32, (1, SUBLANES, 1), 1) == 0) & (step == 0)

        def lru_inputs(k):
            ch = slice(k * MXU_DIM, (k + 1) * MXU_DIM)
            g = jnp.dot(xlb_ref[:, ch], wg_ref[k], preferred_element_type=f32)
            ga = g[:, :MXU_DIM].reshape(bsz, tc, MXU_DIM)
            gx = g[:, MXU_DIM:].reshape(bsz, tc, MXU_DIM)
            a = jnp.exp2(jnp.tanh(ga + half_ab[:, :, ch]) * half_rate[:, :, ch] + half_rate[:, :, ch])
            i = 0.5 * jnp.tanh(gx + half_xb[:, :, ch]) + 0.5
            v = 1.0 - a * a
            mult = jnp.where(v > 0.0, v * lax.rsqrt(v), 0.0)
            mult_head = jnp.where(first_row, 1.0, mult[:, :SUBLANES, :])
            mult = jnp.concatenate([mult_head, mult[:, SUBLANES:, :]], axis=1)
            bx = mult * (i * xl[:, :, ch])
            for b in range(bsz):
                for j in range(MXU_DIM // LANES):
                    s = k * (MXU_DIM // LANES) + j
                    rows_b = pl.ds(b * SCAN_PITCH, tc)
                    a_slab[s, rows_b, :] = a[b, :, j * LANES:(j + 1) * LANES]
                    b_slab[s, rows_b, :] = bx[b, :, j * LANES:(j + 1) * LANES]

        lru_inputs(0)
        u_b = proj(2 * D_LRU, D_CONV)
        mlp_out_piece(1, 0)
        lru_inputs(1)
        u_ly = proj(D_LRU, D_LRU)
        mlp_out_piece(1, 1)
        y_c =(u_b * conv).reshape(rows, D_CONV)
        sqc_ref[...] = (y_c * y_c).astype(bf16)
        half_ly = 0.5 * u_ly
        gelu_ly = half_ly * jnp.tanh(u_ly * ((GELU_C0 * GELU_C1) * (u_ly * u_ly) + GELU_C0)) + half_ly
        pool = pool_ref[...]
        y_c = y_c * lax.rsqrt(_head_mean(sqc_ref, pool) + EPS) * cog_ref[...]
        yb_ref[:, D_LRU:] = y_c.astype(bf16)
        mixed_c = jnp.dot(yb_ref[:, D_LRU:], w_out_ref[D_LRU:, :], preferred_element_type=f32)

        mlp_out_piece(2)

        def scan_step(t, carry):
            new = []
            for s in range(N_SLABS):
                idx = pl.ds(t, bsz, stride=SCAN_PITCH)
                h_t = a_slab[s, idx, :] * carry[s] + b_slab[s, idx, :]
                h_slab[s, idx, :] = h_t
                new.append(h_t)
            return tuple(new)

        h0 = tuple(h_state[:, s * LANES:(s + 1) * LANES] for s in range(N_SLABS))
        h_last = lax.fori_loop(0, tc, scan_step, h0, unroll=True)
        for s in range(N_SLABS):
            h_state[:, s * LANES:(s + 1) * LANES] = h_last[s]
        hl = jnp.stack([
            jnp.concatenate([h_slab[s, pl.ds(b * SCAN_PITCH, tc), :] for s in range(N_SLABS)], axis=1)
            for b in range(bsz)], axis=0)
        y_l = (gelu_ly * hl).reshape(rows, D_LRU)
        sql_ref[...] = (y_l * y_l).astype(bf16)
        y_l = y_l * lax.rsqrt(_head_mean(sql_ref, pool) + EPS) * log_ref[...]
        yb_ref[:, :D_LRU] = y_l.astype(bf16)

        mixed = mixed_c + jnp.dot(yb_ref[:, :D_LRU], w_out_ref[:D_LRU, :], preferred_element_type=f32)
        mlp_out_piece(3)
        x1_prev = x1_ref[...]
        x1 = x_ref[...] + gate1 * mixed.reshape(bsz, tc, d)
        x1_ref[...] = x1
        h2 = x1 * _rms_scale(x1) * (n2g_ref[...][None] * (1.0 + scale2)) + shift2
        h2_ref[...] = h2.reshape(rows, d).astype(bf16)

        for k in range(D_FF // FF_CHUNK):
            cols = slice(k * FF_CHUNK, (k + 1) * FF_CHUNK)
            hk = jnp.dot(h2_ref[...], w1_ref[:, cols], preferred_element_type=f32)
            hid_ref[:, cols] = jnp.square(jnp.maximum(hk, 0.0)).astype(bf16)
            if k == 0:
                x2 = x1_prev + gate2 * mlp_ref[...].reshape(bsz, tc, d)
                o_ref[...] = x2 * _rms_scale(x2) * fg_ref[...][None]

    @pl.when(step == 0)
    def _():
        lx_tail[...] = jnp.zeros_like(lx_tail)
        cv_tail[...] = jnp.zeros_like(cv_tail)
        h_state[...] = jnp.zeros_like(h_state)
        x1_ref[...] = jnp.zeros_like(x1_ref)
        hid_ref[...] = jnp.zeros_like(hid_ref)

        sc = jax.nn.silu(c_ref[...]).astype(bf16)
        mod_acc[...] = jnp.broadcast_to(ada_b_ref[...], mod_acc.shape)

        def ada_chunk(r, c, chunk):
            part = jnp.dot(sc[:, r:r + chunk.shape[0]], chunk[...].astype(bf16),
                           preferred_element_type=f32)
            mod_acc[:, c:c + chunk.shape[1]] += part

        def cast_into(dst):
            def consume(r, c, chunk):
                dst[r:r + chunk.shape[0], c:c + chunk.shape[1]] = chunk[...].astype(dst.dtype)
            return consume

        _stream_from_hbm([(ada_w_hbm, ada_chunk), (w_in_hbm, cast_into(w_in_ref)),
                          (w_out_hbm, cast_into(w_out_ref)), (w1_hbm, cast_into(w1_ref)),
                          (w2_hbm, cast_into(w2_ref))], stage_ref, stage_sems)
        for k in range(N_ADA):
            for b in range(bsz):
                mod_ref[k, b] = mod_acc[b:b + 1, k * d:(k + 1) * d]

        col = lax.broadcasted_iota(jnp.int32, (HEAD_DIM, MXU_DIM), 1)
        row = lax.broadcasted_iota(jnp.int32, (HEAD_DIM, MXU_DIM), 0)
        expand = jnp.where((col & (HEAD_DIM - 1)) == row, 1.0, 0.0).astype(bf16)
        head_of_row = lax.broadcasted_iota(jnp.int32, (MXU_DIM, MXU_DIM), 0) >> HEAD_SHIFT
        head_of_col = lax.broadcasted_iota(jnp.int32, (MXU_DIM, MXU_DIM), 1) >> HEAD_SHIFT
        for t in range(D_LRU // MXU_DIM):
            for g, w_ref in enumerate((gaw_ref, gxw_ref)):
                w4 = (0.5 * w_ref[t * MXU_DIM:(t + 1) * MXU_DIM, :]).astype(bf16)
                tiled = jnp.dot(w4, expand, preferred_element_type=f32)
                wg_ref[t, :, g * MXU_DIM:(g + 1) * MXU_DIM] = jnp.where(
                    head_of_row == head_of_col, tiled, 0.0).astype(bf16)

    last_step = pl.num_programs(0) - 1

    @pl.when(step < last_step)
    def _():
        run(mix=True)

    @pl.when(step == last_step)
    def _():
        run(mix=False)


def _head_pool_matrix():
    blocks = np.kron(np.eye(HEADS_PER_MXU_TILE), np.full((HEAD_DIM, HEAD_DIM), 1.0 / HEAD_DIM))
    return jnp.asarray(blocks, dtype=jnp.bfloat16)


def _const_spec(shape):
    if shape is None:
        return pl.BlockSpec(memory_space=pl.ANY)
    return pl.BlockSpec(shape, lambda s: (0,) * len(shape), pipeline_mode=pl.Buffered(1))


@jax.jit
def kernel(x, c, ada_w, ada_b, norm1_g, w_in, lru_conv_w, lru_conv_b, gate_a_w, gate_a_b, gate_x_w,
           gate_x_b, a_param, short_conv_w, lru_out_g, conv_out_g, w_out, norm2_g, w_mlp1, w_mlp2,
           final_g):
    bsz, seq, d = x.shape
    assert (d, seq % TIME_CHUNK, bsz) == (D_MODEL, 0, SUBLANES)
    assert ada_w.shape[0] == 1, "one layer"

    operands = [
        (c, (bsz, d)),
        (ada_w.reshape(d, N_ADA * d), None),
        (ada_b, (1, N_ADA * d)),
        (norm1_g, (1, d)),
        (w_in.reshape(d, D_IN), None),
        (lru_conv_w.reshape(4, D_LRU), (4, D_LRU)),
        (lru_conv_b, (1, D_LRU)),
        (gate_a_w.reshape(D_LRU, HEAD_DIM), (D_LRU, HEAD_DIM)),
        (gate_x_w.reshape(D_LRU, HEAD_DIM), (D_LRU, HEAD_DIM)),
        (gate_a_b, (1, D_LRU)),
        (gate_x_b, (1, D_LRU)),
        (a_param, (1, D_LRU)),
        (short_conv_w.reshape(3, D_CONV), (3, D_CONV)),
        (lru_out_g, (1, D_LRU)),
        (conv_out_g, (1, D_CONV)),
        (_head_pool_matrix(), (MXU_DIM, MXU_DIM)),
        (w_out.reshape(D_LRU + D_CONV, d), None),
        (norm2_g, (1, d)),
        (w_mlp1.reshape(d, D_FF), None),
        (w_mlp2.reshape(D_FF, d), None),
        (final_g.reshape(1, d), (1, d)),
    ]
    n_chunks = seq // TIME_CHUNK
    block = (bsz, TIME_CHUNK, d)
    rows = bsz * TIME_CHUNK
    return pl.pallas_call(
        _block_kernel,
        grid=(n_chunks + 1,),
        in_specs=[pl.BlockSpec(block, lambda s: (0, jnp.minimum(s, n_chunks - 1), 0))]
        + [_const_spec(shape) for _, shape in operands],
        out_specs=pl.BlockSpec(block, lambda s: (0, jnp.maximum(s - 1, 0), 0)),
        out_shape=jax.ShapeDtypeStruct(x.shape, x.dtype),
        scratch_shapes=[
            pltpu.VMEM((bsz, SUBLANES, D_LRU), jnp.float32),
            pltpu.VMEM((bsz, SUBLANES, D_CONV), jnp.float32),
            pltpu.VMEM((bsz, D_LRU), jnp.float32),
            pltpu.VMEM((N_SLABS, bsz * SCAN_PITCH, LANES), jnp.float32),
            pltpu.VMEM((N_SLABS, bsz * SCAN_PITCH, LANES), jnp.float32),
            pltpu.VMEM((N_SLABS, bsz * SCAN_PITCH, LANES), jnp.float32),
            pltpu.VMEM((rows, d), jnp.bfloat16),
            pltpu.VMEM(block, jnp.float32),
            pltpu.VMEM((rows, d), jnp.bfloat16),
            pltpu.VMEM((rows, D_FF), jnp.bfloat16),
            pltpu.VMEM((rows, d), jnp.float32),
            pltpu.VMEM((rows, D_LRU), jnp.bfloat16),
            pltpu.VMEM((rows, D_LRU), jnp.bfloat16),
            pltpu.VMEM((rows, D_CONV), jnp.bfloat16),
            pltpu.VMEM((rows, D_LRU + D_CONV), jnp.bfloat16),
            pltpu.VMEM((d, D_IN), jnp.bfloat16),
            pltpu.VMEM((D_LRU + D_CONV, d), jnp.bfloat16),
            pltpu.VMEM((d, D_FF), jnp.bfloat16),
            pltpu.VMEM((D_FF, d), jnp.bfloat16),
            pltpu.VMEM((STAGE_SLOTS, STAGE_ROWS, STAGE_COLS), jnp.float32),
            pltpu.SemaphoreType.DMA((STAGE_SLOTS,)),
            pltpu.VMEM((bsz, N_ADA * d), jnp.float32),
            pltpu.VMEM((N_ADA, bsz, 1, d), jnp.float32),
            pltpu.VMEM((D_LRU // MXU_DIM, MXU_DIM, 2 * MXU_DIM), jnp.bfloat16),
        ],
        compiler_params=pltpu.CompilerParams(
            dimension_semantics=("arbitrary",), vmem_limit_bytes=_vmem_limit_bytes(bsz, d)),
        name="hybrid_block",
    )(x, *[v for v, _ in operands])
```

```python
import jax
import jax.numpy as jnp
import numpy as np
from jax import lax
from jax.experimental import pallas as pl
from jax.experimental.pallas import tpu as pltpu

D_MODEL = 1024
D_LRU = 512
D_CONV = 512
HEAD_DIM = 64
HEAD_SHIFT = 6
D_FF = 4096
D_IN = 2 * D_LRU + 3 * D_CONV
N_ADA = 6
C_GATE = 8.0
EPS = 1e-6
LOG2_E = 1.4426950408889634
GELU_C0 = 0.7978845608028654
GELU_C1 = 0.044715

LANES = 128
SUBLANES = 8
MXU_DIM = 256
HEADS_PER_MXU_TILE = MXU_DIM // HEAD_DIM

TIME_CHUNK = 64
FF_CHUNK = 1024
STAGE_ROWS, STAGE_COLS = 256, 512
STAGE_SLOTS = 8
SCAN_PITCH = TIME_CHUNK + SUBLANES
N_SLABS = D_LRU // LANES


V7X_VMEM_BYTES = 64 * 1024 * 1024
COMPILER_TEMP_BYTES = 4 * 1024 * 1024


def _vmem_limit_bytes(bsz, d):
    rows, f32, bf16 = bsz * TIME_CHUNK, 4, 2
    weights = (d * D_IN + (D_LRU + D_CONV) * d + 2 * d * D_FF + 2 * MXU_DIM * 2 * MXU_DIM) * bf16
    windows = 2 * 2 * rows * d * f32
    staging = STAGE_SLOTS * STAGE_ROWS * STAGE_COLS * f32
    slabs = 3 * N_SLABS * bsz * SCAN_PITCH * LANES * f32
    per_chunk_f32 = 2 * rows * d * f32
    per_chunk_bf16 = rows * (2 * d + D_FF + 2 * D_LRU + D_CONV + D_LRU + D_CONV) * bf16
    small = 2 * bsz * N_ADA * d * f32 + 2 * D_LRU * LANES * f32
    total = (weights + windows + staging + slabs + per_chunk_f32 + per_chunk_bf16 + small
             + COMPILER_TEMP_BYTES)
    assert total <= V7X_VMEM_BYTES, total
    return total


def _rms_scale(v):
    return lax.rsqrt(jnp.mean(v * v, axis=-1, keepdims=True) + EPS)


def _time_shift(u, prev_tail, j):
    rolled = pltpu.roll(u, j, axis=1)
    rolled_tail = pltpu.roll(prev_tail, j, axis=1)
    t_idx = lax.broadcasted_iota(jnp.int32, (1, SUBLANES, 1), 1)
    head = jnp.where(t_idx < j, rolled_tail, rolled[:, :SUBLANES, :])
    return jnp.concatenate([head, rolled[:, SUBLANES:, :]], axis=1)


def _head_mean(sq_ref, pool):
    halves = [
        jnp.dot(sq_ref[:, k * MXU_DIM:(k + 1) * MXU_DIM], pool, preferred_element_type=jnp.float32)
        for k in range(sq_ref.shape[1] // MXU_DIM)
    ]
    return jnp.concatenate(halves, axis=1)


def _stream_from_hbm(sources, stage, sems):
    n_slots, stage_rows, stage_cols = stage.shape
    chunks = []
    for src, consume in sources:
        n_rows, n_cols = src.shape
        assert n_rows % stage_rows == 0 and n_cols % stage_cols == 0
        for r in range(0, n_rows, stage_rows):
            for c in range(0, n_cols, stage_cols):
                chunks.append((src, consume, r, c))

    def copy(i):
        src, _, r, c = chunks[i]
        slot = i % n_slots
        return pltpu.make_async_copy(src.at[pl.ds(r, stage_rows), pl.ds(c, stage_cols)],
                                     stage.at[slot], sems.at[slot])

    lookahead = n_slots - 1
    for i in range(min(lookahead, len(chunks))):
        copy(i).start()
    for i, (_, consume, r, c) in enumerate(chunks):
        if i + lookahead < len(chunks):
            copy(i + lookahead).start()
        copy(i).wait()
        consume(r, c, stage.at[i % n_slots])


def _block_kernel(x_ref, c_ref, ada_w_hbm, ada_b_ref, n1g_ref, w_in_hbm, lcw_ref, lcb_ref, gaw_ref,
                  gxw_ref, gab_ref, gxb_ref, ap_ref, scw_ref, log_ref, cog_ref, pool_ref, w_out_hbm, n2g_ref,
                  w1_hbm, w2_hbm, fg_ref, o_ref,
                  lx_tail, cv_tail, h_state, a_slab, b_slab, h_slab, hb_ref, x1_ref, h2_ref, hid_ref, x2_ref,
                  xlb_ref, sql_ref, sqc_ref, yb_ref,
                  w_in_ref, w_out_ref, w1_ref, w2_ref, stage_ref, stage_sems, mod_acc, mod_ref, wg_ref):
    step = pl.program_id(0)
    bsz, tc, d = x_ref.shape
    rows = bsz * tc
    f32, bf16 = jnp.float32, jnp.bfloat16

    def run(mix):
        shift1, scale1, gate1, shift2, scale2, gate2 = [mod_ref[k] for k in range(N_ADA)]

        def mlp_out_piece(k, k_half=None):
            cols = slice(k * MXU_DIM, (k + 1) * MXU_DIM)

            def with_residual(mlp_cols):
                mlp_cols = mlp_cols.reshape(bsz, tc, MXU_DIM)
                return (x1_ref[:, :, cols] + gate2[:, :, cols] * mlp_cols).reshape(rows, MXU_DIM)

            if k_half is None:
                x2_ref[:, cols] = with_residual(
                    jnp.dot(hid_ref[...], w2_ref[:, cols], preferred_element_type=f32))
                return
            feats = slice(k_half * (D_FF // 2), (k_half + 1) * (D_FF // 2))
            part = jnp.dot(hid_ref[:, feats], w2_ref[feats, cols], preferred_element_type=f32)
            if k_half == 0:
                x2_ref[:, cols] = part
            else:
                x2_ref[:, cols] = with_residual(x2_ref[:, cols] + part)

        def proj(k, width):
            u = jnp.dot(hb_ref[...], w_in_ref[:, k:k + width], preferred_element_type=f32)
            return u.reshape(bsz, tc, width)

        mlp_out_piece(0)
        if not mix:
            for k in range(1, d // MXU_DIM):
                mlp_out_piece(k)
            x2 = x2_ref[...].reshape(bsz, tc, d)
            o_ref[...] = x2 * _rms_scale(x2) * fg_ref[...][None]
            return
        x = x_ref[...]
        h = x * _rms_scale(x) * (n1g_ref[...][None] * (1.0 + scale1)) + shift1
        hb_ref[...] = h.reshape(rows, d).astype(bf16)

        u_lx = proj(0, D_LRU)
        u_c = proj(2 * D_LRU + D_CONV, D_CONV)
        u_v = proj(2 * D_LRU + 2 * D_CONV, D_CONV)
        lcw = lcw_ref[...]
        prev_lx = lx_tail[...]
        xl = u_lx * lcw[3][None, None, :] + lcb_ref[...][None]
        for j in range(1, 4):
            xl = xl + _time_shift(u_lx, prev_lx, j) * lcw[3 - j][None, None, :]
        lx_tail[...] = u_lx[:, tc - SUBLANES:, :]
        xlb_ref[...] = xl.reshape(rows, D_LRU).astype(bf16)
        scw = scw_ref[...]
        cv = u_c * u_v
        prev_cv = cv_tail[...]
        conv = cv * scw[2][None, None, :]
        for j in range(1, 3):
            conv = conv + _time_shift(cv, prev_cv, j) * scw[2 - j][None, None, :]
        cv_tail[...] = cv[:, tc - SUBLANES:, :]

        ap = ap_ref[...]
        softplus = jnp.maximum(ap, 0.0) + jnp.log1p(jnp.exp(-jnp.abs(ap)))
        half_rate = ((-0.5 * C_GATE * LOG2_E) * softplus)[None]
        half_ab = (0.5 * gab_ref[...])[None]
        half_xb = (0.5 * gxb_ref[...])[None]
        first_row = (lax.broadcasted_iota(jnp.int32, (1, SUBLANES, 1), 1) == 0) & (step == 0)

        def lru_inputs(k):
            ch = slice(k * MXU_DIM, (k + 1) * MXU_DIM)
            g = jnp.dot(xlb_ref[:, ch], wg_ref[k], preferred_element_type=f32)
            ga = g[:, :MXU_DIM].reshape(bsz, tc, MXU_DIM)
            gx = g[:, MXU_DIM:].reshape(bsz, tc, MXU_DIM)
            a = jnp.exp2(jnp.tanh(ga + half_ab[:, :, ch]) * half_rate[:, :, ch] + half_rate[:, :, ch])
            i = 0.5 * jnp.tanh(gx + half_xb[:, :, ch]) + 0.5
            v = 1.0 - a * a
            mult = jnp.where(v > 0.0, v * lax.rsqrt(v), 0.0)
            mult_head = jnp.where(first_row, 1.0, mult[:, :SUBLANES, :])
            mult = jnp.concatenate([mult_head, mult[:, SUBLANES:, :]], axis=1)
            bx = mult * (i * xl[:, :, ch])
            for b in range(bsz):
                for j in range(MXU_DIM // LANES):
                    s = k * (MXU_DIM // LANES) + j
                    rows_b = pl.ds(b * SCAN_PITCH, tc)
                    a_slab[s, rows_b, :] = a[b, :, j * LANES:(j + 1) * LANES]
                    b_slab[s, rows_b, :] = bx[b, :, j * LANES:(j + 1) * LANES]

        lru_inputs(0)
        u_b = proj(2 * D_LRU, D_CONV)
        mlp_out_piece(1, 0)
        lru_inputs(1)
        u_ly = proj(D_LRU, D_LRU)
        mlp_out_piece(1, 1)
        y_c =(u_b * conv).reshape(rows, D_CONV)
        sqc_ref[...] = (y_c * y_c).astype(bf16)
        half_ly = 0.5 * u_ly
        gelu_ly = half_ly * jnp.tanh(u_ly * ((GELU_C0 * GELU_C1) * (u_ly * u_ly) + GELU_C0)) + half_ly
        pool = pool_ref[...]
        y_c = y_c * lax.rsqrt(_head_mean(sqc_ref, pool) + EPS) * cog_ref[...]
        yb_ref[:, D_LRU:] = y_c.astype(bf16)
        mixed_c = jnp.dot(yb_ref[:, D_LRU:], w_out_ref[D_LRU:, :], preferred_element_type=f32)

        mlp_out_piece(2)

        def scan_step(t, carry):
            new = []
            for s in range(N_SLABS):
                idx = pl.ds(t, bsz, stride=SCAN_PITCH)
                h_t = a_slab[s, idx, :] * carry[s] + b_slab[s, idx, :]
                h_slab[s, idx, :] = h_t
                new.append(h_t)
            return tuple(new)

        h0 = tuple(h_state[:, s * LANES:(s + 1) * LANES] for s in range(N_SLABS))
        h_last = lax.fori_loop(0, tc, scan_step, h0, unroll=True)
        for s in range(N_SLABS):
            h_state[:, s * LANES:(s + 1) * LANES] = h_last[s]
        hl = jnp.stack([
            jnp.concatenate([h_slab[s, pl.ds(b * SCAN_PITCH, tc), :] for s in range(N_SLABS)], axis=1)
            for b in range(bsz)], axis=0)
        y_l = (gelu_ly * hl).reshape(rows, D_LRU)
        sql_ref[...] = (y_l * y_l).astype(bf16)
        y_l = y_l * lax.rsqrt(_head_mean(sql_ref, pool) + EPS) * log_ref[...]
        yb_ref[:, :D_LRU] = y_l.astype(bf16)

        mixed = mixed_c + jnp.dot(yb_ref[:, :D_LRU], w_out_ref[:D_LRU, :], preferred_element_type=f32)
        mlp_out_piece(3)
        x1 = x_ref[...] + gate1 * mixed.reshape(bsz, tc, d)
        x1_ref[...] = x1
        h2 = x1 * _rms_scale(x1) * (n2g_ref[...][None] * (1.0 + scale2)) + shift2
        h2_ref[...] = h2.reshape(rows, d).astype(bf16)

        for k in range(D_FF // FF_CHUNK):
            cols = slice(k * FF_CHUNK, (k + 1) * FF_CHUNK)
            hk = jnp.dot(h2_ref[...], w1_ref[:, cols], preferred_element_type=f32)
            hid_ref[:, cols] = jnp.square(jnp.maximum(hk, 0.0)).astype(bf16)
            if k == 0:
                x2 = x2_ref[...].reshape(bsz, tc, d)
                o_ref[...] = x2 * _rms_scale(x2) * fg_ref[...][None]

    @pl.when(step == 0)
    def _():
        lx_tail[...] = jnp.zeros_like(lx_tail)
        cv_tail[...] = jnp.zeros_like(cv_tail)
        h_state[...] = jnp.zeros_like(h_state)
        x1_ref[...] = jnp.zeros_like(x1_ref)
        hid_ref[...] = jnp.zeros_like(hid_ref)

        sc = jax.nn.silu(c_ref[...]).astype(bf16)
        mod_acc[...] = jnp.broadcast_to(ada_b_ref[...], mod_acc.shape)

        def ada_chunk(r, c, chunk):
            part = jnp.dot(sc[:, r:r + chunk.shape[0]], chunk[...].astype(bf16),
                           preferred_element_type=f32)
            mod_acc[:, c:c + chunk.shape[1]] += part

        def cast_into(dst):
            def consume(r, c, chunk):
                dst[r:r + chunk.shape[0], c:c + chunk.shape[1]] = chunk[...].astype(dst.dtype)
            return consume

        _stream_from_hbm([(ada_w_hbm, ada_chunk), (w_in_hbm, cast_into(w_in_ref)),
                          (w_out_hbm, cast_into(w_out_ref)), (w1_hbm, cast_into(w1_ref)),
                          (w2_hbm, cast_into(w2_ref))], stage_ref, stage_sems)
        for k in range(N_ADA):
            for b in range(bsz):
                mod_ref[k, b] = mod_acc[b:b + 1, k * d:(k + 1) * d]

        col = lax.broadcasted_iota(jnp.int32, (HEAD_DIM, MXU_DIM), 1)
        row = lax.broadcasted_iota(jnp.int32, (HEAD_DIM, MXU_DIM), 0)
        expand = jnp.where((col & (HEAD_DIM - 1)) == row, 1.0, 0.0).astype(bf16)
        head_of_row = lax.broadcasted_iota(jnp.int32, (MXU_DIM, MXU_DIM), 0) >> HEAD_SHIFT
        head_of_col = lax.broadcasted_iota(jnp.int32, (MXU_DIM, MXU_DIM), 1) >> HEAD_SHIFT
        for t in range(D_LRU // MXU_DIM):
            for g, w_ref in enumerate((gaw_ref, gxw_ref)):
                w4 = (0.5 * w_ref[t * MXU_DIM:(t + 1) * MXU_DIM, :]).astype(bf16)
                tiled = jnp.dot(w4, expand, preferred_element_type=f32)
                wg_ref[t, :, g * MXU_DIM:(g + 1) * MXU_DIM] = jnp.where(
                    head_of_row == head_of_col, tiled, 0.0).astype(bf16)

    last_step = pl.num_programs(0) - 1

    @pl.when(step < last_step)
    def _():
        run(mix=True)

    @pl.when(step == last_step)
    def _():
        run(mix=False)


def _head_pool_matrix():
    blocks = np.kron(np.eye(HEADS_PER_MXU_TILE), np.full((HEAD_DIM, HEAD_DIM), 1.0 / HEAD_DIM))
    return jnp.asarray(blocks, dtype=jnp.bfloat16)


def _const_spec(shape):
    if shape is None:
        return pl.BlockSpec(memory_space=pl.ANY)
    return pl.BlockSpec(shape, lambda s: (0,) * len(shape), pipeline_mode=pl.Buffered(1))


@jax.jit
def kernel(x, c, ada_w, ada_b, norm1_g, w_in, lru_conv_w, lru_conv_b, gate_a_w, gate_a_b, gate_x_w,
           gate_x_b, a_param, short_conv_w, lru_out_g, conv_out_g, w_out, norm2_g, w_mlp1, w_mlp2,
           final_g):
    bsz, seq, d = x.shape
    assert (d, seq % TIME_CHUNK, bsz) == (D_MODEL, 0, SUBLANES)
    assert ada_w.shape[0] == 1, "one layer"

    operands = [
        (c, (bsz, d)),
        (ada_w.reshape(d, N_ADA * d), None),
        (ada_b, (1, N_ADA * d)),
        (norm1_g, (1, d)),
        (w_in.reshape(d, D_IN), None),
        (lru_conv_w.reshape(4, D_LRU), (4, D_LRU)),
        (lru_conv_b, (1, D_LRU)),
        (gate_a_w.reshape(D_LRU, HEAD_DIM), (D_LRU, HEAD_DIM)),
        (gate_x_w.reshape(D_LRU, HEAD_DIM), (D_LRU, HEAD_DIM)),
        (gate_a_b, (1, D_LRU)),
        (gate_x_b, (1, D_LRU)),
        (a_param, (1, D_LRU)),
        (short_conv_w.reshape(3, D_CONV), (3, D_CONV)),
        (lru_out_g, (1, D_LRU)),
        (conv_out_g, (1, D_CONV)),
        (_head_pool_matrix(), (MXU_DIM, MXU_DIM)),
        (w_out.reshape(D_LRU + D_CONV, d), None),
        (norm2_g, (1, d)),
        (w_mlp1.reshape(d, D_FF), None),
        (w_mlp2.reshape(D_FF, d), None),
        (final_g.reshape(1, d), (1, d)),
    ]
    n_chunks = seq // TIME_CHUNK
    block = (bsz, TIME_CHUNK, d)
    rows = bsz * TIME_CHUNK
    return pl.pallas_call(
        _block_kernel,
        grid=(n_chunks + 1,),
        in_specs=[pl.BlockSpec(block, lambda s: (0, jnp.minimum(s, n_chunks - 1), 0))]
        + [_const_spec(shape) for _, shape in operands],
        out_specs=pl.BlockSpec(block, lambda s: (0, jnp.maximum(s - 1, 0), 0)),
        out_shape=jax.ShapeDtypeStruct(x.shape, x.dtype),
        scratch_shapes=[
            pltpu.VMEM((bsz, SUBLANES, D_LRU), jnp.float32),
            pltpu.VMEM((bsz, SUBLANES, D_CONV), jnp.float32),
            pltpu.VMEM((bsz, D_LRU), jnp.float32),
            pltpu.VMEM((N_SLABS, bsz * SCAN_PITCH, LANES), jnp.float32),
            pltpu.VMEM((N_SLABS, bsz * SCAN_PITCH, LANES), jnp.float32),
            pltpu.VMEM((N_SLABS, bsz * SCAN_PITCH, LANES), jnp.float32),
            pltpu.VMEM((rows, d), jnp.bfloat16),
            pltpu.VMEM(block, jnp.float32),
            pltpu.VMEM((rows, d), jnp.bfloat16),
            pltpu.VMEM((rows, D_FF), jnp.bfloat16),
            pltpu.VMEM((rows, d), jnp.float32),
            pltpu.VMEM((rows, D_LRU), jnp.bfloat16),
            pltpu.VMEM((rows, D_LRU), jnp.bfloat16),
            pltpu.VMEM((rows, D_CONV), jnp.bfloat16),
            pltpu.VMEM((rows, D_LRU + D_CONV), jnp.bfloat16),
            pltpu.VMEM((d, D_IN), jnp.bfloat16),
            pltpu.VMEM((D_LRU + D_CONV, d), jnp.bfloat16),
            pltpu.VMEM((d, D_FF), jnp.bfloat16),
            pltpu.VMEM((D_FF, d), jnp.bfloat16),
            pltpu.VMEM((STAGE_SLOTS, STAGE_ROWS, STAGE_COLS), jnp.float32),
            pltpu.SemaphoreType.DMA((STAGE_SLOTS,)),
            pltpu.VMEM((bsz, N_ADA * d), jnp.float32),
            pltpu.VMEM((N_ADA, bsz, 1, d), jnp.float32),
            pltpu.VMEM((D_LRU // MXU_DIM, MXU_DIM, 2 * MXU_DIM), jnp.bfloat16),
        ],
        compiler_params=pltpu.CompilerParams(
            dimension_semantics=("arbitrary",), vmem_limit_bytes=_vmem_limit_bytes(bsz, d)),
        name="hybrid_block",
    )(x, *[v for v, _ in operands])
```

```python
import jax
import jax.numpy as jnp
import numpy as np
from jax import lax
from jax.experimental import pallas as pl
from jax.experimental.pallas import tpu as pltpu

D_MODEL = 1024
D_LRU = 512
D_CONV = 512
HEAD_DIM = 64
HEAD_SHIFT = 6
D_FF = 4096
D_IN = 2 * D_LRU + 3 * D_CONV
N_ADA = 6
C_GATE = 8.0
EPS = 1e-6
LOG2_E = 1.4426950408889634
GELU_C0 = 0.7978845608028654
GELU_C1 = 0.044715

LANES = 128
SUBLANES = 8
MXU_DIM = 256
HEADS_PER_MXU_TILE = MXU_DIM // HEAD_DIM

TIME_CHUNK = 64
FF_CHUNK = 1024
STAGE_ROWS, STAGE_COLS = 256, 512
STAGE_SLOTS = 8
SCAN_PITCH = TIME_CHUNK + SUBLANES
N_SLABS = D_LRU // LANES


V7X_VMEM_BYTES = 64 * 1024 * 1024
COMPILER_TEMP_BYTES = 4 * 1024 * 1024


def _vmem_limit_bytes(bsz, d):
    rows, f32, bf16 = bsz * TIME_CHUNK, 4, 2
    weights = (d * D_IN + (D_LRU + D_CONV) * d + 2 * d * D_FF + 2 * MXU_DIM * 2 * MXU_DIM) * bf16
    windows = 2 * 2 * rows * d * f32
    staging = STAGE_SLOTS * STAGE_ROWS * STAGE_COLS * f32
    slabs = 3 * N_SLABS * bsz * SCAN_PITCH * LANES * f32
    per_chunk_f32 = 2 * rows * d * f32
    per_chunk_bf16 = rows * (2 * d + D_FF + 2 * D_LRU + D_CONV + D_LRU + D_CONV) * bf16
    small = 2 * bsz * N_ADA * d * f32 + 2 * D_LRU * LANES * f32
    total = (weights + windows + staging + slabs + per_chunk_f32 + per_chunk_bf16 + small
             + COMPILER_TEMP_BYTES)
    assert total <= V7X_VMEM_BYTES, total
    return total


def _rms_scale(v):
    return lax.rsqrt(jnp.mean(v * v, axis=-1, keepdims=True) + EPS)


def _time_shift(u, prev_tail, j):
    rolled = pltpu.roll(u, j, axis=1)
    rolled_tail = pltpu.roll(prev_tail, j, axis=1)
    t_idx = lax.broadcasted_iota(jnp.int32, (1, SUBLANES, 1), 1)
    head = jnp.where(t_idx < j, rolled_tail, rolled[:, :SUBLANES, :])
    return jnp.concatenate([head, rolled[:, SUBLANES:, :]], axis=1)


def _head_mean(sq_ref, pool):
    halves = [
        jnp.dot(sq_ref[:, k * MXU_DIM:(k + 1) * MXU_DIM], pool, preferred_element_type=jnp.float32)
        for k in range(sq_ref.shape[1] // MXU_DIM)
    ]
    return jnp.concatenate(halves, axis=1)


def _stream_from_hbm(sources, stage, sems):
    n_slots, stage_rows, stage_cols = stage.shape
    chunks = []
    for src, consume in sources:
        n_rows, n_cols = src.shape
        assert n_rows % stage_rows == 0 and n_cols % stage_cols == 0
        for r in range(0, n_rows, stage_rows):
            for c in range(0, n_cols, stage_cols):
                chunks.append((src, consume, r, c))

    def copy(i):
        src, _, r, c = chunks[i]
        slot = i % n_slots
        return pltpu.make_async_copy(src.at[pl.ds(r, stage_rows), pl.ds(c, stage_cols)],
                                     stage.at[slot], sems.at[slot])

    lookahead = n_slots - 1
    for i in range(min(lookahead, len(chunks))):
        copy(i).start()
    for i, (_, consume, r, c) in enumerate(chunks):
        if i + lookahead < len(chunks):
            copy(i + lookahead).start()
        copy(i).wait()
        consume(r, c, stage.at[i % n_slots])


def _block_kernel(x_ref, c_ref, ada_w_hbm, ada_b_ref, n1g_ref, w_in_hbm, lcw_ref, lcb_ref, gaw_ref,
                  gxw_ref, gab_ref, gxb_ref, ap_ref, scw_ref, log_ref, cog_ref, pool_ref, w_out_hbm, n2g_ref,
                  w1_hbm, w2_hbm, fg_ref, o_ref,
                  lx_tail, cv_tail, h_state, a_slab, b_slab, h_slab, hb_ref, x1_ref, h2_ref, hid_ref, x2_ref,
                  xlb_ref, sql_ref, sqc_ref, yb_ref,
                  w_in_ref, w_out_ref, w1_ref, w2_ref, stage_ref, stage_sems, mod_acc, mod_ref, wg_ref):
    step = pl.program_id(0)
    bsz, tc, d = x_ref.shape
    rows = bsz * tc
    f32, bf16 = jnp.float32, jnp.bfloat16

    def run(mix):
        shift1, scale1, gate1, shift2, scale2, gate2 = [mod_ref[k] for k in range(N_ADA)]

        def mlp_out_piece(k, k_half=None):
            cols = slice(k * MXU_DIM, (k + 1) * MXU_DIM)

            def with_residual(mlp_cols):
                mlp_cols = mlp_cols.reshape(bsz, tc, MXU_DIM)
                return (x1_ref[:, :, cols] + gate2[:, :, cols] * mlp_cols).reshape(rows, MXU_DIM)

            if k_half is None:
                x2_ref[:, cols] = with_residual(
                    jnp.dot(hid_ref[...], w2_ref[:, cols], preferred_element_type=f32))
                return
            feats = slice(k_half * (D_FF // 2), (k_half + 1) * (D_FF // 2))
            part = jnp.dot(hid_ref[:, feats], w2_ref[feats, cols], preferred_element_type=f32)
            if k_half == 0:
                x2_ref[:, cols] = part
            else:
                x2_ref[:, cols] = with_residual(x2_ref[:, cols] + part)

        def proj(k, width):
            u = jnp.dot(hb_ref[...], w_in_ref[:, k:k + width], preferred_element_type=f32)
            return u.reshape(bsz, tc, width)

        mlp_out_piece(0)
        if not mix:
            for k in range(1, d // MXU_DIM):
                mlp_out_piece(k)
            x2 = x2_ref[...].reshape(bsz, tc, d)
            o_ref[...] = x2 * _rms_scale(x2) * fg_ref[...][None]
            return
        x = x_ref[...]
        h = x * _rms_scale(x) * (n1g_ref[...][None] * (1.0 + scale1)) + shift1
        hb_ref[...] = h.reshape(rows, d).astype(bf16)

        u_lx = proj(0, D_LRU)
        u_c = proj(2 * D_LRU + D_CONV, D_CONV)
        u_v = proj(2 * D_LRU + 2 * D_CONV, D_CONV)
        lcw = lcw_ref[...]
        prev_lx = lx_tail[...]
        xl = u_lx * lcw[3][None, None, :] + lcb_ref[...][None]
        for j in range(1, 4):
            xl = xl + _time_shift(u_lx, prev_lx, j) * lcw[3 - j][None, None, :]
        lx_tail[...] = u_lx[:, tc - SUBLANES:, :]
        xlb_ref[...] = xl.reshape(rows, D_LRU).astype(bf16)
        scw = scw_ref[...]
        cv = u_c * u_v
        prev_cv = cv_tail[...]
        conv = cv * scw[2][None, None, :]
        for j in range(1, 3):
            conv = conv + _time_shift(cv, prev_cv, j) * scw[2 - j][None, None, :]
        cv_tail[...] = cv[:, tc - SUBLANES:, :]

        ap = ap_ref[...]
        softplus = jnp.maximum(ap, 0.0) + jnp.log1p(jnp.exp(-jnp.abs(ap)))
        half_rate = ((-0.5 * C_GATE * LOG2_E) * softplus)[None]
        half_ab = (0.5 * gab_ref[...])[None]
        half_xb = (0.5 * gxb_ref[...])[None]
        first_row = (lax.broadcasted_iota(jnp.int32, (1, SUBLANES, 1), 1) == 0) & (step == 0)

        def lru_inputs(k):
            ch = slice(k * MXU_DIM, (k + 1) * MXU_DIM)
            g = jnp.dot(xlb_ref[:, ch], wg_ref[k], preferred_element_type=f32)
            ga = g[:, :MXU_DIM].reshape(bsz, tc, MXU_DIM)
            gx = g[:, MXU_DIM:].reshape(bsz, tc, MXU_DIM)
            a = jnp.exp2(jnp.tanh(ga + half_ab[:, :, ch]) * half_rate[:, :, ch] + half_rate[:, :, ch])
            i = 0.5 * jnp.tanh(gx + half_xb[:, :, ch]) + 0.5
            v = 1.0 - a * a
            mult = jnp.where(v > 0.0, v * lax.rsqrt(v), 0.0)
            mult_head = jnp.where(first_row, 1.0, mult[:, :SUBLANES, :])
            mult = jnp.concatenate([mult_head, mult[:, SUBLANES:, :]], axis=1)
            bx = mult * (i * xl[:, :, ch])
            for b in range(bsz):
                for j in range(MXU_DIM // LANES):
                    s = k * (MXU_DIM // LANES) + j
                    rows_b = pl.ds(b * SCAN_PITCH, tc)
                    a_slab[s, rows_b, :] = a[b, :, j * LANES:(j + 1) * LANES]
                    b_slab[s, rows_b, :] = bx[b, :, j * LANES:(j + 1) * LANES]

        lru_inputs(0)
        u_b = proj(2 * D_LRU, D_CONV)
        mlp_out_piece(1, 0)
        lru_inputs(1)
        u_ly = proj(D_LRU, D_LRU)
        mlp_out_piece(1, 1)
        y_c = (u_b * conv).reshape(rows, D_CONV)
        sqc_ref[...] = (y_c * y_c).astype(bf16)
        half_ly = 0.5 * u_ly
        gelu_ly = half_ly * jnp.tanh(u_ly * ((GELU_C0 * GELU_C1) * (u_ly * u_ly) + GELU_C0)) + half_ly
        pool = pool_ref[...]
        y_c = y_c * lax.rsqrt(_head_mean(sqc_ref, pool) + EPS) * cog_ref[...]
        yb_ref[:, D_LRU:] = y_c.astype(bf16)
        mixed_c = jnp.dot(yb_ref[:, D_LRU:], w_out_ref[D_LRU:, :], preferred_element_type=f32)

        mlp_out_piece(2)

        def scan_step(t, carry):
            new = []
            for s in range(N_SLABS):
                idx = pl.ds(t, bsz, stride=SCAN_PITCH)
                h_t = a_slab[s, idx, :] * carry[s] + b_slab[s, idx, :]
                h_slab[s, idx, :] = h_t
                new.append(h_t)
            return tuple(new)

        h0 = tuple(h_state[:, s * LANES:(s + 1) * LANES] for s in range(N_SLABS))
        h_last = lax.fori_loop(0, tc, scan_step, h0, unroll=True)
        for s in range(N_SLABS):
            h_state[:, s * LANES:(s + 1) * LANES] = h_last[s]
        hl = jnp.stack([
            jnp.concatenate([h_slab[s, pl.ds(b * SCAN_PITCH, tc), :] for s in range(N_SLABS)], axis=1)
            for b in range(bsz)], axis=0)
        y_l = (gelu_ly * hl).reshape(rows, D_LRU)
        sql_ref[...] = (y_l * y_l).astype(bf16)
        y_l = y_l * lax.rsqrt(_head_mean(sql_ref, pool) + EPS) * log_ref[...]
        yb_ref[:, :D_LRU] = y_l.astype(bf16)

        mixed = mixed_c + jnp.dot(yb_ref[:, :D_LRU], w_out_ref[:D_LRU, :], preferred_element_type=f32)
        mlp_out_piece(3)
        x1 = x_ref[...] + gate1 * mixed.reshape(bsz, tc, d)
        x1_ref[...] = x1
        h2 = x1 * _rms_scale(x1) * (n2g_ref[...][None] * (1.0 + scale2)) + shift2
        h2_ref[...] = h2.reshape(rows, d).astype(bf16)

        for k in range(D_FF // FF_CHUNK):
            cols = slice(k * FF_CHUNK, (k + 1) * FF_CHUNK)
            hk = jnp.dot(h2_ref[...], w1_ref[:, cols], preferred_element_type=f32)
            hid_ref[:, cols] = jnp.square(jnp.maximum(hk, 0.0)).astype(bf16)
            if k == 0:
                x2 = x2_ref[...].reshape(bsz, tc, d)
                o_ref[...] = x2 * _rms_scale(x2) * fg_ref[...][None]

    @pl.when(step == 0)
    def _():
        lx_tail[...] = jnp.zeros_like(lx_tail)
        cv_tail[...] = jnp.zeros_like(cv_tail)
        h_state[...] = jnp.zeros_like(h_state)
        x1_ref[...] = jnp.zeros_like(x1_ref)
        hid_ref[...] = jnp.zeros_like(hid_ref)

        sc = jax.nn.silu(c_ref[...]).astype(bf16)
        mod_acc[...] = jnp.broadcast_to(ada_b_ref[...], mod_acc.shape)

        def ada_chunk(r, c, chunk):
            part = jnp.dot(sc[:, r:r + chunk.shape[0]], chunk[...].astype(bf16),
                           preferred_element_type=f32)
            mod_acc[:, c:c + chunk.shape[1]] += part

        def cast_into(dst):
            def consume(r, c, chunk):
                dst[r:r + chunk.shape[0], c:c + chunk.shape[1]] = chunk[...].astype(dst.dtype)
            return consume

        _stream_from_hbm([(ada_w_hbm, ada_chunk), (w_in_hbm, cast_into(w_in_ref)),
                          (w_out_hbm, cast_into(w_out_ref)), (w1_hbm, cast_into(w1_ref)),
                          (w2_hbm, cast_into(w2_ref))], stage_ref, stage_sems)
        for k in range(N_ADA):
            for b in range(bsz):
                mod_ref[k, b] = mod_acc[b:b + 1, k * d:(k + 1) * d]

        col = lax.broadcasted_iota(jnp.int32, (HEAD_DIM, MXU_DIM), 1)
        row = lax.broadcasted_iota(jnp.int32, (HEAD_DIM, MXU_DIM), 0)
        expand = jnp.where((col & (HEAD_DIM - 1)) == row, 1.0, 0.0).astype(bf16)
        head_of_row = lax.broadcasted_iota(jnp.int32, (MXU_DIM, MXU_DIM), 0) >> HEAD_SHIFT
        head_of_col = lax.broadcasted_iota(jnp.int32, (MXU_DIM, MXU_DIM), 1) >> HEAD_SHIFT
        for t in range(D_LRU // MXU_DIM):
            for g, w_ref in enumerate((gaw_ref, gxw_ref)):
                w4 = (0.5 * w_ref[t * MXU_DIM:(t + 1) * MXU_DIM, :]).astype(bf16)
                tiled = jnp.dot(w4, expand, preferred_element_type=f32)
                wg_ref[t, :, g * MXU_DIM:(g + 1) * MXU_DIM] = jnp.where(
                    head_of_row == head_of_col, tiled, 0.0).astype(bf16)

    last_step = pl.num_programs(0) - 1

    @pl.when(step < last_step)
    def _():
        run(mix=True)

    @pl.when(step == last_step)
    def _():
        run(mix=False)


def _head_pool_matrix():
    blocks = np.kron(np.eye(HEADS_PER_MXU_TILE), np.full((HEAD_DIM, HEAD_DIM), 1.0 / HEAD_DIM))
    return jnp.asarray(blocks, dtype=jnp.bfloat16)


def _const_spec(shape):
    if shape is None:
        return pl.BlockSpec(memory_space=pl.ANY)
    return pl.BlockSpec(shape, lambda s: (0,) * len(shape), pipeline_mode=pl.Buffered(1))


@jax.jit
def kernel(x, c, ada_w, ada_b, norm1_g, w_in, lru_conv_w, lru_conv_b, gate_a_w, gate_a_b, gate_x_w,
           gate_x_b, a_param, short_conv_w, lru_out_g, conv_out_g, w_out, norm2_g, w_mlp1, w_mlp2,
           final_g):
    bsz, seq, d = x.shape
    assert (d, seq % TIME_CHUNK, bsz) == (D_MODEL, 0, SUBLANES)
    assert ada_w.shape[0] == 1, "one layer"

    operands = [
        (c, (bsz, d)),
        (ada_w.reshape(d, N_ADA * d), None),
        (ada_b, (1, N_ADA * d)),
        (norm1_g, (1, d)),
        (w_in.reshape(d, D_IN), None),
        (lru_conv_w.reshape(4, D_LRU), (4, D_LRU)),
        (lru_conv_b, (1, D_LRU)),
        (gate_a_w.reshape(D_LRU, HEAD_DIM), (D_LRU, HEAD_DIM)),
        (gate_x_w.reshape(D_LRU, HEAD_DIM), (D_LRU, HEAD_DIM)),
        (gate_a_b, (1, D_LRU)),
        (gate_x_b, (1, D_LRU)),
        (a_param, (1, D_LRU)),
        (short_conv_w.reshape(3, D_CONV), (3, D_CONV)),
        (lru_out_g, (1, D_LRU)),
        (conv_out_g, (1, D_CONV)),
        (_head_pool_matrix(), (MXU_DIM, MXU_DIM)),
        (w_out.reshape(D_LRU + D_CONV, d), None),
        (norm2_g, (1, d)),
        (w_mlp1.reshape(d, D_FF), None),
        (w_mlp2.reshape(D_FF, d), None),
        (final_g.reshape(1, d), (1, d)),
    ]
    n_chunks = seq // TIME_CHUNK
    block = (bsz, TIME_CHUNK, d)
    rows = bsz * TIME_CHUNK
    return pl.pallas_call(
        _block_kernel,
        grid=(n_chunks + 1,),
        in_specs=[pl.BlockSpec(block, lambda s: (0, jnp.minimum(s, n_chunks - 1), 0))]
        + [_const_spec(shape) for _, shape in operands],
        out_specs=pl.BlockSpec(block, lambda s: (0, jnp.maximum(s - 1, 0), 0)),
        out_shape=jax.ShapeDtypeStruct(x.shape, x.dtype),
        scratch_shapes=[
            pltpu.VMEM((bsz, SUBLANES, D_LRU), jnp.float32),
            pltpu.VMEM((bsz, SUBLANES, D_CONV), jnp.float32),
            pltpu.VMEM((bsz, D_LRU), jnp.float32),
            pltpu.VMEM((N_SLABS, bsz * SCAN_PITCH, LANES), jnp.float32),
            pltpu.VMEM((N_SLABS, bsz * SCAN_PITCH, LANES), jnp.float32),
            pltpu.VMEM((N_SLABS, bsz * SCAN_PITCH, LANES), jnp.float32),
            pltpu.VMEM((rows, d), jnp.bfloat16),
            pltpu.VMEM(block, jnp.float32),
            pltpu.VMEM((rows, d), jnp.bfloat16),
            pltpu.VMEM((rows, D_FF), jnp.bfloat16),
            pltpu.VMEM((rows, d), jnp.float32),
            pltpu.VMEM((rows, D_LRU), jnp.bfloat16),
            pltpu.VMEM((rows, D_LRU), jnp.bfloat16),
            pltpu.VMEM((rows, D_CONV), jnp.bfloat16),
            pltpu.VMEM((rows, D_LRU + D_CONV), jnp.bfloat16),
            pltpu.VMEM((d, D_IN), jnp.bfloat16),
            pltpu.VMEM((D_LRU + D_CONV, d), jnp.bfloat16),
            pltpu.VMEM((d, D_FF), jnp.bfloat16),
            pltpu.VMEM((D_FF, d), jnp.bfloat16),
            pltpu.VMEM((STAGE_SLOTS, STAGE_ROWS, STAGE_COLS), jnp.float32),
            pltpu.SemaphoreType.DMA((STAGE_SLOTS,)),
            pltpu.VMEM((bsz, N_ADA * d), jnp.float32),
            pltpu.VMEM((N_ADA, bsz, 1, d), jnp.float32),
            pltpu.VMEM((D_LRU // MXU_DIM, MXU_DIM, 2 * MXU_DIM), jnp.bfloat16),
        ],
        compiler_params=pltpu.CompilerParams(
            dimension_semantics=("arbitrary",), vmem_limit_bytes=_vmem_limit_bytes(bsz, d)),
        name="hybrid_block",
    )(x, *[v for v, _ in operands])
```

```python
import jax
import jax.numpy as jnp
import numpy as np
from jax import lax
from jax.experimental import pallas as pl
from jax.experimental.pallas import tpu as pltpu

D_MODEL = 1024
D_LRU = 512
D_CONV = 512
HEAD_DIM = 64
HEAD_SHIFT = 6
D_FF = 4096
D_IN = 2 * D_LRU + 3 * D_CONV
N_ADA = 6
C_GATE = 8.0
EPS = 1e-6
LOG2_E = 1.4426950408889634
GELU_C0 = 0.7978845608028654
GELU_C1 = 0.044715

LANES = 128
SUBLANES = 8
MXU_DIM = 256
HEADS_PER_MXU_TILE = MXU_DIM // HEAD_DIM

TIME_CHUNK = 64
FF_CHUNK = 1024
STAGE_ROWS, STAGE_COLS = 256, 512
STAGE_SLOTS = 8
X_SLOTS = 3
SCAN_PITCH = TIME_CHUNK + SUBLANES
N_SLABS = D_LRU // LANES


V7X_VMEM_BYTES = 64 * 1024 * 1024
COMPILER_TEMP_BYTES = 4 * 1024 * 1024


def _vmem_limit_bytes(bsz, d):
    rows, f32, bf16 = bsz * TIME_CHUNK, 4, 2
    weights = (d * D_IN + (D_LRU + D_CONV) * d + 2 * d * D_FF + 2 * MXU_DIM * 2 * MXU_DIM) * bf16
    windows = (X_SLOTS + 2) * rows * d * f32
    staging = STAGE_SLOTS * STAGE_ROWS * STAGE_COLS * f32
    slabs = 3 * N_SLABS * bsz * SCAN_PITCH * LANES * f32
    per_chunk_f32 = 2 * rows * d * f32
    per_chunk_bf16 = rows * (2 * d + D_FF + 2 * D_LRU + D_CONV + D_LRU + D_CONV) * bf16
    small = 2 * bsz * N_ADA * d * f32 + 2 * D_LRU * LANES * f32
    total = (weights + windows + staging + slabs + per_chunk_f32 + per_chunk_bf16 + small
             + COMPILER_TEMP_BYTES)
    assert total <= V7X_VMEM_BYTES, total
    return total


def _rms_scale(v):
    return lax.rsqrt(jnp.mean(v * v, axis=-1, keepdims=True) + EPS)


def _time_shift(u, prev_tail, j):
    rolled = pltpu.roll(u, j, axis=1)
    rolled_tail = pltpu.roll(prev_tail, j, axis=1)
    t_idx = lax.broadcasted_iota(jnp.int32, (1, SUBLANES, 1), 1)
    head = jnp.where(t_idx < j, rolled_tail, rolled[:, :SUBLANES, :])
    return jnp.concatenate([head, rolled[:, SUBLANES:, :]], axis=1)


def _head_mean(sq_ref, pool):
    halves = [
        jnp.dot(sq_ref[:, k * MXU_DIM:(k + 1) * MXU_DIM], pool, preferred_element_type=jnp.float32)
        for k in range(sq_ref.shape[1] // MXU_DIM)
    ]
    return jnp.concatenate(halves, axis=1)


def _stream_from_hbm(sources, stage, sems):
    n_slots, stage_rows, stage_cols = stage.shape
    chunks = []
    for src, consume in sources:
        n_rows, n_cols = src.shape
        assert n_rows % stage_rows == 0 and n_cols % stage_cols == 0
        for r in range(0, n_rows, stage_rows):
            for c in range(0, n_cols, stage_cols):
                chunks.append((src, consume, r, c))

    def copy(i):
        src, _, r, c = chunks[i]
        slot = i % n_slots
        return pltpu.make_async_copy(src.at[pl.ds(r, stage_rows), pl.ds(c, stage_cols)],
                                     stage.at[slot], sems.at[slot])

    lookahead = n_slots - 1
    for i in range(min(lookahead, len(chunks))):
        copy(i).start()
    for i, (_, consume, r, c) in enumerate(chunks):
        if i + lookahead < len(chunks):
            copy(i + lookahead).start()
        copy(i).wait()
        consume(r, c, stage.at[i % n_slots])


def _block_kernel(x_hbm, c_ref, ada_w_hbm, ada_b_ref, n1g_ref, w_in_hbm, lcw_ref, lcb_ref, gaw_ref,
                  gxw_ref, gab_ref, gxb_ref, ap_ref, scw_ref, log_ref, cog_ref, pool_ref, w_out_hbm, n2g_ref,
                  w1_hbm, w2_hbm, fg_ref, o_ref,
                  lx_tail, cv_tail, h_state, a_slab, b_slab, h_slab, hb_ref, x1_ref, h2_ref, hid_ref, x2_ref,
                  xlb_ref, sql_ref, sqc_ref, yb_ref,
                  w_in_ref, w_out_ref, w1_ref, w2_ref, stage_ref, stage_sems, mod_acc, mod_ref, wg_ref,
                  x_ring, x_sems):
    step = pl.program_id(0)
    n_slots, bsz, tc, d = x_ring.shape
    n_chunks = pl.num_programs(0) - 1
    x_ref = x_ring.at[step % n_slots]

    def x_copy(chunk):
        slot = chunk % n_slots
        return pltpu.make_async_copy(x_hbm.at[:, pl.ds(chunk * tc, tc), :], x_ring.at[slot],
                                     x_sems.at[slot])
    rows = bsz * tc
    f32, bf16 = jnp.float32, jnp.bfloat16

    def run(mix):
        shift1, scale1, gate1, shift2, scale2, gate2 = [mod_ref[k] for k in range(N_ADA)]

        def mlp_out_piece(k, k_half=None):
            cols = slice(k * MXU_DIM, (k + 1) * MXU_DIM)

            def with_residual(mlp_cols):
                mlp_cols = mlp_cols.reshape(bsz, tc, MXU_DIM)
                return (x1_ref[:, :, cols] + gate2[:, :, cols] * mlp_cols).reshape(rows, MXU_DIM)

            if k_half is None:
                x2_ref[:, cols] = with_residual(
                    jnp.dot(hid_ref[...], w2_ref[:, cols], preferred_element_type=f32))
                return
            feats = slice(k_half * (D_FF // 2), (k_half + 1) * (D_FF // 2))
            part = jnp.dot(hid_ref[:, feats], w2_ref[feats, cols], preferred_element_type=f32)
            if k_half == 0:
                x2_ref[:, cols] = part
            else:
                x2_ref[:, cols] = with_residual(x2_ref[:, cols] + part)

        def proj(k, width):
            u = jnp.dot(hb_ref[...], w_in_ref[:, k:k + width], preferred_element_type=f32)
            return u.reshape(bsz, tc, width)

        mlp_out_piece(0)
        if not mix:
            for k in range(1, d // MXU_DIM):
                mlp_out_piece(k)
            x2 = x2_ref[...].reshape(bsz, tc, d)
            o_ref[...] = x2 * _rms_scale(x2) * fg_ref[...][None]
            return
        x = x_ref[...]
        h = x * _rms_scale(x) * (n1g_ref[...][None] * (1.0 + scale1)) + shift1
        hb_ref[...] = h.reshape(rows, d).astype(bf16)

        u_lx = proj(0, D_LRU)
        u_c = proj(2 * D_LRU + D_CONV, D_CONV)
        u_v = proj(2 * D_LRU + 2 * D_CONV, D_CONV)
        lcw = lcw_ref[...]
        prev_lx = lx_tail[...]
        xl = u_lx * lcw[3][None, None, :] + lcb_ref[...][None]
        for j in range(1, 4):
            xl = xl + _time_shift(u_lx, prev_lx, j) * lcw[3 - j][None, None, :]
        lx_tail[...] = u_lx[:, tc - SUBLANES:, :]
        xlb_ref[...] = xl.reshape(rows, D_LRU).astype(bf16)
        scw = scw_ref[...]
        cv = u_c * u_v
        prev_cv = cv_tail[...]
        conv = cv * scw[2][None, None, :]
        for j in range(1, 3):
            conv = conv + _time_shift(cv, prev_cv, j) * scw[2 - j][None, None, :]
        cv_tail[...] = cv[:, tc - SUBLANES:, :]

        ap = ap_ref[...]
        softplus = jnp.maximum(ap, 0.0) + jnp.log1p(jnp.exp(-jnp.abs(ap)))
        half_rate = ((-0.5 * C_GATE * LOG2_E) * softplus)[None]
        half_ab = (0.5 * gab_ref[...])[None]
        half_xb = (0.5 * gxb_ref[...])[None]
        first_row = (lax.broadcasted_iota(jnp.int32, (1, SUBLANES, 1), 1) == 0) & (step == 0)

        def lru_inputs(k):
            ch = slice(k * MXU_DIM, (k + 1) * MXU_DIM)
            g = jnp.dot(xlb_ref[:, ch], wg_ref[k], preferred_element_type=f32)
            ga = g[:, :MXU_DIM].reshape(bsz, tc, MXU_DIM)
            gx = g[:, MXU_DIM:].reshape(bsz, tc, MXU_DIM)
            a = jnp.exp2(jnp.tanh(ga + half_ab[:, :, ch]) * half_rate[:, :, ch] + half_rate[:, :, ch])
            i = 0.5 * jnp.tanh(gx + half_xb[:, :, ch]) + 0.5
            v = 1.0 - a * a
            mult = jnp.where(v > 0.0, v * lax.rsqrt(v), 0.0)
            mult_head = jnp.where(first_row, 1.0, mult[:, :SUBLANES, :])
            mult = jnp.concatenate([mult_head, mult[:, SUBLANES:, :]], axis=1)
            bx = mult * (i * xl[:, :, ch])
            for b in range(bsz):
                for j in range(MXU_DIM // LANES):
                    s = k * (MXU_DIM // LANES) + j
                    rows_b = pl.ds(b * SCAN_PITCH, tc)
                    a_slab[s, rows_b, :] = a[b, :, j * LANES:(j + 1) * LANES]
                    b_slab[s, rows_b, :] = bx[b, :, j * LANES:(j + 1) * LANES]

        lru_inputs(0)
        u_b = proj(2 * D_LRU, D_CONV)
        mlp_out_piece(1, 0)
        lru_inputs(1)
        u_ly = proj(D_LRU, D_LRU)
        mlp_out_piece(1, 1)
        y_c = (u_b * conv).reshape(rows, D_CONV)
        sqc_ref[...] = (y_c * y_c).astype(bf16)
        half_ly = 0.5 * u_ly
        gelu_ly = half_ly * jnp.tanh(u_ly * ((GELU_C0 * GELU_C1) * (u_ly * u_ly) + GELU_C0)) + half_ly
        pool = pool_ref[...]
        y_c = y_c * lax.rsqrt(_head_mean(sqc_ref, pool) + EPS) * cog_ref[...]
        yb_ref[:, D_LRU:] = y_c.astype(bf16)
        mixed_c = jnp.dot(yb_ref[:, D_LRU:], w_out_ref[D_LRU:, :], preferred_element_type=f32)

        mlp_out_piece(2)

        def scan_step(t, carry):
            new = []
            for s in range(N_SLABS):
                idx = pl.ds(t, bsz, stride=SCAN_PITCH)
                h_t = a_slab[s, idx, :] * carry[s] + b_slab[s, idx, :]
                h_slab[s, idx, :] = h_t
                new.append(h_t)
            return tuple(new)

        h0 = tuple(h_state[:, s * LANES:(s + 1) * LANES] for s in range(N_SLABS))
        h_last = lax.fori_loop(0, tc, scan_step, h0, unroll=True)
        for s in range(N_SLABS):
            h_state[:, s * LANES:(s + 1) * LANES] = h_last[s]
        hl = jnp.stack([
            jnp.concatenate([h_slab[s, pl.ds(b * SCAN_PITCH, tc), :] for s in range(N_SLABS)], axis=1)
            for b in range(bsz)], axis=0)
        y_l = (gelu_ly * hl).reshape(rows, D_LRU)
        sql_ref[...] = (y_l * y_l).astype(bf16)
        y_l = y_l * lax.rsqrt(_head_mean(sql_ref, pool) + EPS) * log_ref[...]
        yb_ref[:, :D_LRU] = y_l.astype(bf16)

        mixed = mixed_c + jnp.dot(yb_ref[:, :D_LRU], w_out_ref[:D_LRU, :], preferred_element_type=f32)
        mlp_out_piece(3)
        x1 = x_ref[...] + gate1 * mixed.reshape(bsz, tc, d)
        x1_ref[...] = x1
        h2 = x1 * _rms_scale(x1) * (n2g_ref[...][None] * (1.0 + scale2)) + shift2
        h2_ref[...] = h2.reshape(rows, d).astype(bf16)

        for k in range(D_FF // FF_CHUNK):
            cols = slice(k * FF_CHUNK, (k + 1) * FF_CHUNK)
            hk = jnp.dot(h2_ref[...], w1_ref[:, cols], preferred_element_type=f32)
            hid_ref[:, cols] = jnp.square(jnp.maximum(hk, 0.0)).astype(bf16)
            if k == 0:
                x2 = x2_ref[...].reshape(bsz, tc, d)
                o_ref[...] = x2 * _rms_scale(x2) * fg_ref[...][None]

    @pl.when(step == 0)
    def _():
        for chunk in range(n_slots - 1):
            x_copy(chunk).start()
        lx_tail[...] = jnp.zeros_like(lx_tail)
        cv_tail[...] = jnp.zeros_like(cv_tail)
        h_state[...] = jnp.zeros_like(h_state)
        x1_ref[...] = jnp.zeros_like(x1_ref)
        hid_ref[...] = jnp.zeros_like(hid_ref)

        sc = jax.nn.silu(c_ref[...]).astype(bf16)
        mod_acc[...] = jnp.broadcast_to(ada_b_ref[...], mod_acc.shape)

        def ada_chunk(r, c, chunk):
            part = jnp.dot(sc[:, r:r + chunk.shape[0]], chunk[...].astype(bf16),
                           preferred_element_type=f32)
            mod_acc[:, c:c + chunk.shape[1]] += part

        def cast_into(dst):
            def consume(r, c, chunk):
                dst[r:r + chunk.shape[0], c:c + chunk.shape[1]] = chunk[...].astype(dst.dtype)
            return consume

        _stream_from_hbm([(ada_w_hbm, ada_chunk), (w_in_hbm, cast_into(w_in_ref)),
                          (w_out_hbm, cast_into(w_out_ref)), (w1_hbm, cast_into(w1_ref)),
                          (w2_hbm, cast_into(w2_ref))], stage_ref, stage_sems)
        for k in range(N_ADA):
            for b in range(bsz):
                mod_ref[k, b] = mod_acc[b:b + 1, k * d:(k + 1) * d]

        col = lax.broadcasted_iota(jnp.int32, (HEAD_DIM, MXU_DIM), 1)
        row = lax.broadcasted_iota(jnp.int32, (HEAD_DIM, MXU_DIM), 0)
        expand = jnp.where((col & (HEAD_DIM - 1)) == row, 1.0, 0.0).astype(bf16)
        head_of_row = lax.broadcasted_iota(jnp.int32, (MXU_DIM, MXU_DIM), 0) >> HEAD_SHIFT
        head_of_col = lax.broadcasted_iota(jnp.int32, (MXU_DIM, MXU_DIM), 1) >> HEAD_SHIFT
        for t in range(D_LRU // MXU_DIM):
            for g, w_ref in enumerate((gaw_ref, gxw_ref)):
                w4 = (0.5 * w_ref[t * MXU_DIM:(t + 1) * MXU_DIM, :]).astype(bf16)
                tiled = jnp.dot(w4, expand, preferred_element_type=f32)
                wg_ref[t, :, g * MXU_DIM:(g + 1) * MXU_DIM] = jnp.where(
                    head_of_row == head_of_col, tiled, 0.0).astype(bf16)

    last_step = pl.num_programs(0) - 1

    @pl.when(step + (n_slots - 1) < n_chunks)
    def _():
        x_copy(step + (n_slots - 1)).start()

    @pl.when(step < last_step)
    def _():
        x_copy(step).wait()
        run(mix=True)

    @pl.when(step == last_step)
    def _():
        run(mix=False)


def _head_pool_matrix():
    blocks = np.kron(np.eye(HEADS_PER_MXU_TILE), np.full((HEAD_DIM, HEAD_DIM), 1.0 / HEAD_DIM))
    return jnp.asarray(blocks, dtype=jnp.bfloat16)


def _const_spec(shape):
    if shape is None:
        return pl.BlockSpec(memory_space=pl.ANY)
    return pl.BlockSpec(shape, lambda s: (0,) * len(shape), pipeline_mode=pl.Buffered(1))


@jax.jit
def kernel(x, c, ada_w, ada_b, norm1_g, w_in, lru_conv_w, lru_conv_b, gate_a_w, gate_a_b, gate_x_w,
           gate_x_b, a_param, short_conv_w, lru_out_g, conv_out_g, w_out, norm2_g, w_mlp1, w_mlp2,
           final_g):
    bsz, seq, d = x.shape
    assert (d, seq % TIME_CHUNK, bsz) == (D_MODEL, 0, SUBLANES)
    assert ada_w.shape[0] == 1, "one layer"

    operands = [
        (c, (bsz, d)),
        (ada_w.reshape(d, N_ADA * d), None),
        (ada_b, (1, N_ADA * d)),
        (norm1_g, (1, d)),
        (w_in.reshape(d, D_IN), None),
        (lru_conv_w.reshape(4, D_LRU), (4, D_LRU)),
        (lru_conv_b, (1, D_LRU)),
        (gate_a_w.reshape(D_LRU, HEAD_DIM), (D_LRU, HEAD_DIM)),
        (gate_x_w.reshape(D_LRU, HEAD_DIM), (D_LRU, HEAD_DIM)),
        (gate_a_b, (1, D_LRU)),
        (gate_x_b, (1, D_LRU)),
        (a_param, (1, D_LRU)),
        (short_conv_w.reshape(3, D_CONV), (3, D_CONV)),
        (lru_out_g, (1, D_LRU)),
        (conv_out_g, (1, D_CONV)),
        (_head_pool_matrix(), (MXU_DIM, MXU_DIM)),
        (w_out.reshape(D_LRU + D_CONV, d), None),
        (norm2_g, (1, d)),
        (w_mlp1.reshape(d, D_FF), None),
        (w_mlp2.reshape(D_FF, d), None),
        (final_g.reshape(1, d), (1, d)),
    ]
    n_chunks = seq // TIME_CHUNK
    block = (bsz, TIME_CHUNK, d)
    rows = bsz * TIME_CHUNK
    return pl.pallas_call(
        _block_kernel,
        grid=(n_chunks + 1,),
        in_specs=[pl.BlockSpec(memory_space=pl.ANY)]
        + [_const_spec(shape) for _, shape in operands],
        out_specs=pl.BlockSpec(block, lambda s: (0, jnp.maximum(s - 1, 0), 0)),
        out_shape=jax.ShapeDtypeStruct(x.shape, x.dtype),
        scratch_shapes=[
            pltpu.VMEM((bsz, SUBLANES, D_LRU), jnp.float32),
            pltpu.VMEM((bsz, SUBLANES, D_CONV), jnp.float32),
            pltpu.VMEM((bsz, D_LRU), jnp.float32),
            pltpu.VMEM((N_SLABS, bsz * SCAN_PITCH, LANES), jnp.float32),
            pltpu.VMEM((N_SLABS, bsz * SCAN_PITCH, LANES), jnp.float32),
            pltpu.VMEM((N_SLABS, bsz * SCAN_PITCH, LANES), jnp.float32),
            pltpu.VMEM((rows, d), jnp.bfloat16),
            pltpu.VMEM(block, jnp.float32),
            pltpu.VMEM((rows, d), jnp.bfloat16),
            pltpu.VMEM((rows, D_FF), jnp.bfloat16),
            pltpu.VMEM((rows, d), jnp.float32),
            pltpu.VMEM((rows, D_LRU), jnp.bfloat16),
            pltpu.VMEM((rows, D_LRU), jnp.bfloat16),
            pltpu.VMEM((rows, D_CONV), jnp.bfloat16),
            pltpu.VMEM((rows, D_LRU + D_CONV), jnp.bfloat16),
            pltpu.VMEM((d, D_IN), jnp.bfloat16),
            pltpu.VMEM((D_LRU + D_CONV, d), jnp.bfloat16),
            pltpu.VMEM((d, D_FF), jnp.bfloat16),
            pltpu.VMEM((D_FF, d), jnp.bfloat16),
            pltpu.VMEM((STAGE_SLOTS, STAGE_ROWS, STAGE_COLS), jnp.float32),
            pltpu.SemaphoreType.DMA((STAGE_SLOTS,)),
            pltpu.VMEM((bsz, N_ADA * d), jnp.float32),
            pltpu.VMEM((N_ADA, bsz, 1, d), jnp.float32),
            pltpu.VMEM((D_LRU // MXU_DIM, MXU_DIM, 2 * MXU_DIM), jnp.bfloat16),
            pltpu.VMEM((X_SLOTS,) + block, jnp.float32),
            pltpu.SemaphoreType.DMA((X_SLOTS,)),
        ],
        compiler_params=pltpu.CompilerParams(
            dimension_semantics=("arbitrary",), vmem_limit_bytes=_vmem_limit_bytes(bsz, d)),
        name="hybrid_block",
    )(x, *[v for v, _ in operands])
```

```python
import jax
import jax.numpy as jnp
import numpy as np
from jax import lax
from jax.experimental import pallas as pl
from jax.experimental.pallas import tpu as pltpu

D_MODEL = 1024
D_LRU = 512
D_CONV = 512
HEAD_DIM = 64
HEAD_SHIFT = 6
D_FF = 4096
D_IN = 2 * D_LRU + 3 * D_CONV
N_ADA = 6
C_GATE = 8.0
EPS = 1e-6
LOG2_E = 1.4426950408889634
GELU_C0 = 0.7978845608028654
GELU_C1 = 0.044715

LANES = 128
SUBLANES = 8
MXU_DIM = 256
HEADS_PER_MXU_TILE = MXU_DIM // HEAD_DIM

TIME_CHUNK = 64
FF_CHUNK = 1024
STAGE_ROWS, STAGE_COLS = 256, 512
STAGE_SLOTS = 8
X_SLOTS = 3
SCAN_PITCH = TIME_CHUNK + SUBLANES
N_SLABS = D_LRU // LANES


V7X_VMEM_BYTES = 64 * 1024 * 1024
COMPILER_TEMP_BYTES = 4 * 1024 * 1024


def _vmem_limit_bytes(bsz, d):
    rows, f32, bf16 = bsz * TIME_CHUNK, 4, 2
    weights = (d * D_IN + (D_LRU + D_CONV) * d + 2 * d * D_FF + 2 * MXU_DIM * 2 * MXU_DIM) * bf16
    windows = (X_SLOTS + 2) * rows * d * f32
    staging = STAGE_SLOTS * STAGE_ROWS * STAGE_COLS * f32
    slabs = 3 * N_SLABS * bsz * SCAN_PITCH * LANES * f32
    per_chunk_f32 = 2 * rows * d * f32
    per_chunk_bf16 = rows * (2 * d + D_FF + 2 * D_LRU + D_CONV + D_LRU + D_CONV) * bf16
    small = 2 * bsz * N_ADA * d * f32 + 2 * D_LRU * LANES * f32
    total = (weights + windows + staging + slabs + per_chunk_f32 + per_chunk_bf16 + small
             + COMPILER_TEMP_BYTES)
    assert total <= V7X_VMEM_BYTES, total
    return total


def _rms_scale(v):
    return lax.rsqrt(jnp.mean(v * v, axis=-1, keepdims=True) + EPS)


def _time_shift(u, prev_tail, j):
    rolled = pltpu.roll(u, j, axis=1)
    rolled_tail = pltpu.roll(prev_tail, j, axis=1)
    t_idx = lax.broadcasted_iota(jnp.int32, (1, SUBLANES, 1), 1)
    head = jnp.where(t_idx < j, rolled_tail, rolled[:, :SUBLANES, :])
    return jnp.concatenate([head, rolled[:, SUBLANES:, :]], axis=1)


def _head_mean(sq_ref, pool):
    halves = [
        jnp.dot(sq_ref[:, k * MXU_DIM:(k + 1) * MXU_DIM], pool, preferred_element_type=jnp.float32)
        for k in range(sq_ref.shape[1] // MXU_DIM)
    ]
    return jnp.concatenate(halves, axis=1)


def _stream_from_hbm(sources, stage, sems):
    n_slots, stage_rows, stage_cols = stage.shape
    chunks = []
    for src, consume in sources:
        n_rows, n_cols = src.shape
        assert n_rows % stage_rows == 0 and n_cols % stage_cols == 0
        for r in range(0, n_rows, stage_rows):
            for c in range(0, n_cols, stage_cols):
                chunks.append((src, consume, r, c))

    def copy(i):
        src, _, r, c = chunks[i]
        slot = i % n_slots
        return pltpu.make_async_copy(src.at[pl.ds(r, stage_rows), pl.ds(c, stage_cols)],
                                     stage.at[slot], sems.at[slot])

    lookahead = n_slots - 1
    for i in range(min(lookahead, len(chunks))):
        copy(i).start()
    for i, (_, consume, r, c) in enumerate(chunks):
        if i + lookahead < len(chunks):
            copy(i + lookahead).start()
        copy(i).wait()
        consume(r, c, stage.at[i % n_slots])


def _block_kernel(x_hbm, c_ref, ada_w_hbm, ada_b_ref, n1g_ref, w_in_hbm, lcw_ref, lcb_ref, gaw_ref,
                  gxw_ref, gab_ref, gxb_ref, ap_ref, scw_ref, log_ref, cog_ref, pool_ref, w_out_hbm, n2g_ref,
                  w1_hbm, w2_hbm, fg_ref, o_hbm,
                  lx_tail, cv_tail, h_state, a_slab, b_slab, h_slab, hb_ref, x1_ref, h2_ref, hid_ref, x2_ref,
                  xlb_ref, sql_ref, sqc_ref, yb_ref,
                  w_in_ref, w_out_ref, w1_ref, w2_ref, stage_ref, stage_sems, mod_acc, mod_ref, wg_ref,
                  x_ring, x_sems, o_ring, o_sems):
    step = pl.program_id(0)
    o_ref = o_ring.at[(step + 1) % 2]

    def o_copy(chunk):
        slot = chunk % 2
        return pltpu.make_async_copy(o_ring.at[slot], o_hbm.at[:, pl.ds(chunk * o_ring.shape[2], o_ring.shape[2]), :],
                                     o_sems.at[slot])

    n_slots, bsz, tc, d = x_ring.shape
    n_chunks = pl.num_programs(0) - 1
    x_ref = x_ring.at[step % n_slots]

    def x_copy(chunk):
        slot = chunk % n_slots
        return pltpu.make_async_copy(x_hbm.at[:, pl.ds(chunk * tc, tc), :], x_ring.at[slot],
                                     x_sems.at[slot])
    rows = bsz * tc
    f32, bf16 = jnp.float32, jnp.bfloat16

    def run(mix):
        shift1, scale1, gate1, shift2, scale2, gate2 = [mod_ref[k] for k in range(N_ADA)]

        def mlp_out_piece(k, k_half=None):
            cols = slice(k * MXU_DIM, (k + 1) * MXU_DIM)

            def with_residual(mlp_cols):
                mlp_cols = mlp_cols.reshape(bsz, tc, MXU_DIM)
                return (x1_ref[:, :, cols] + gate2[:, :, cols] * mlp_cols).reshape(rows, MXU_DIM)

            if k_half is None:
                x2_ref[:, cols] = with_residual(
                    jnp.dot(hid_ref[...], w2_ref[:, cols], preferred_element_type=f32))
                return
            feats = slice(k_half * (D_FF // 2), (k_half + 1) * (D_FF // 2))
            part = jnp.dot(hid_ref[:, feats], w2_ref[feats, cols], preferred_element_type=f32)
            if k_half == 0:
                x2_ref[:, cols] = part
            else:
                x2_ref[:, cols] = with_residual(x2_ref[:, cols] + part)

        def proj(k, width):
            u = jnp.dot(hb_ref[...], w_in_ref[:, k:k + width], preferred_element_type=f32)
            return u.reshape(bsz, tc, width)

        mlp_out_piece(0)
        if not mix:
            for k in range(1, d // MXU_DIM):
                mlp_out_piece(k)
            x2 = x2_ref[...].reshape(bsz, tc, d)
            o_ref[...] = x2 * _rms_scale(x2) * fg_ref[...][None]
            return
        x = x_ref[...]
        h = x * _rms_scale(x) * (n1g_ref[...][None] * (1.0 + scale1)) + shift1
        hb_ref[...] = h.reshape(rows, d).astype(bf16)

        u_lx = proj(0, D_LRU)
        u_c = proj(2 * D_LRU + D_CONV, D_CONV)
        u_v = proj(2 * D_LRU + 2 * D_CONV, D_CONV)
        lcw = lcw_ref[...]
        prev_lx = lx_tail[...]
        xl = u_lx * lcw[3][None, None, :] + lcb_ref[...][None]
        for j in range(1, 4):
            xl = xl + _time_shift(u_lx, prev_lx, j) * lcw[3 - j][None, None, :]
        lx_tail[...] = u_lx[:, tc - SUBLANES:, :]
        xlb_ref[...] = xl.reshape(rows, D_LRU).astype(bf16)
        scw = scw_ref[...]
        cv = u_c * u_v
        prev_cv = cv_tail[...]
        conv = cv * scw[2][None, None, :]
        for j in range(1, 3):
            conv = conv + _time_shift(cv, prev_cv, j) * scw[2 - j][None, None, :]
        cv_tail[...] = cv[:, tc - SUBLANES:, :]

        ap = ap_ref[...]
        softplus = jnp.maximum(ap, 0.0) + jnp.log1p(jnp.exp(-jnp.abs(ap)))
        half_rate = ((-0.5 * C_GATE * LOG2_E) * softplus)[None]
        half_ab = (0.5 * gab_ref[...])[None]
        half_xb = (0.5 * gxb_ref[...])[None]
        first_row = (lax.broadcasted_iota(jnp.int32, (1, SUBLANES, 1), 1) == 0) & (step == 0)

        def lru_inputs(k):
            ch = slice(k * MXU_DIM, (k + 1) * MXU_DIM)
            g = jnp.dot(xlb_ref[:, ch], wg_ref[k], preferred_element_type=f32)
            ga = g[:, :MXU_DIM].reshape(bsz, tc, MXU_DIM)
            gx = g[:, MXU_DIM:].reshape(bsz, tc, MXU_DIM)
            a = jnp.exp2(jnp.tanh(ga + half_ab[:, :, ch]) * half_rate[:, :, ch] + half_rate[:, :, ch])
            i = 0.5 * jnp.tanh(gx + half_xb[:, :, ch]) + 0.5
            v = 1.0 - a * a
            mult = jnp.where(v > 0.0, v * lax.rsqrt(v), 0.0)
            mult_head = jnp.where(first_row, 1.0, mult[:, :SUBLANES, :])
            mult = jnp.concatenate([mult_head, mult[:, SUBLANES:, :]], axis=1)
            bx = mult * (i * xl[:, :, ch])
            for b in range(bsz):
                for j in range(MXU_DIM // LANES):
                    s = k * (MXU_DIM // LANES) + j
                    rows_b = pl.ds(b * SCAN_PITCH, tc)
                    a_slab[s, rows_b, :] = a[b, :, j * LANES:(j + 1) * LANES]
                    b_slab[s, rows_b, :] = bx[b, :, j * LANES:(j + 1) * LANES]

        lru_inputs(0)
        u_b = proj(2 * D_LRU, D_CONV)
        mlp_out_piece(1, 0)
        lru_inputs(1)
        u_ly = proj(D_LRU, D_LRU)
        mlp_out_piece(1, 1)
        y_c = (u_b * conv).reshape(rows, D_CONV)
        sqc_ref[...] = (y_c * y_c).astype(bf16)
        half_ly = 0.5 * u_ly
        gelu_ly = half_ly * jnp.tanh(u_ly * ((GELU_C0 * GELU_C1) * (u_ly * u_ly) + GELU_C0)) + half_ly
        pool = pool_ref[...]
        y_c = y_c * lax.rsqrt(_head_mean(sqc_ref, pool) + EPS) * cog_ref[...]
        yb_ref[:, D_LRU:] = y_c.astype(bf16)
        mixed_c = jnp.dot(yb_ref[:, D_LRU:], w_out_ref[D_LRU:, :], preferred_element_type=f32)

        mlp_out_piece(2)

        def scan_step(t, carry):
            new = []
            for s in range(N_SLABS):
                idx = pl.ds(t, bsz, stride=SCAN_PITCH)
                h_t = a_slab[s, idx, :] * carry[s] + b_slab[s, idx, :]
                h_slab[s, idx, :] = h_t
                new.append(h_t)
            return tuple(new)

        h0 = tuple(h_state[:, s * LANES:(s + 1) * LANES] for s in range(N_SLABS))
        h_last = lax.fori_loop(0, tc, scan_step, h0, unroll=True)
        for s in range(N_SLABS):
            h_state[:, s * LANES:(s + 1) * LANES] = h_last[s]
        hl = jnp.stack([
            jnp.concatenate([h_slab[s, pl.ds(b * SCAN_PITCH, tc), :] for s in range(N_SLABS)], axis=1)
            for b in range(bsz)], axis=0)
        y_l = (gelu_ly * hl).reshape(rows, D_LRU)
        sql_ref[...] = (y_l * y_l).astype(bf16)
        y_l = y_l * lax.rsqrt(_head_mean(sql_ref, pool) + EPS) * log_ref[...]
        yb_ref[:, :D_LRU] = y_l.astype(bf16)

        mixed = mixed_c + jnp.dot(yb_ref[:, :D_LRU], w_out_ref[:D_LRU, :], preferred_element_type=f32)
        mlp_out_piece(3)
        x1 = x_ref[...] + gate1 * mixed.reshape(bsz, tc, d)
        x1_ref[...] = x1
        h2 = x1 * _rms_scale(x1) * (n2g_ref[...][None] * (1.0 + scale2)) + shift2
        h2_ref[...] = h2.reshape(rows, d).astype(bf16)

        for k in range(D_FF // FF_CHUNK):
            cols = slice(k * FF_CHUNK, (k + 1) * FF_CHUNK)
            hk = jnp.dot(h2_ref[...], w1_ref[:, cols], preferred_element_type=f32)
            hid_ref[:, cols] = jnp.square(jnp.maximum(hk, 0.0)).astype(bf16)
            if k == 0:
                x2 = x2_ref[...].reshape(bsz, tc, d)
                o_ref[...] = x2 * _rms_scale(x2) * fg_ref[...][None]

    @pl.when(step == 0)
    def _():
        for chunk in range(n_slots - 1):
            x_copy(chunk).start()
        lx_tail[...] = jnp.zeros_like(lx_tail)
        cv_tail[...] = jnp.zeros_like(cv_tail)
        h_state[...] = jnp.zeros_like(h_state)
        x1_ref[...] = jnp.zeros_like(x1_ref)
        hid_ref[...] = jnp.zeros_like(hid_ref)

        sc = jax.nn.silu(c_ref[...]).astype(bf16)
        mod_acc[...] = jnp.broadcast_to(ada_b_ref[...], mod_acc.shape)

        def ada_chunk(r, c, chunk):
            part = jnp.dot(sc[:, r:r + chunk.shape[0]], chunk[...].astype(bf16),
                           preferred_element_type=f32)
            mod_acc[:, c:c + chunk.shape[1]] += part

        def cast_into(dst):
            def consume(r, c, chunk):
                dst[r:r + chunk.shape[0], c:c + chunk.shape[1]] = chunk[...].astype(dst.dtype)
            return consume

        _stream_from_hbm([(ada_w_hbm, ada_chunk), (w_in_hbm, cast_into(w_in_ref)),
                          (w_out_hbm, cast_into(w_out_ref)), (w1_hbm, cast_into(w1_ref)),
                          (w2_hbm, cast_into(w2_ref))], stage_ref, stage_sems)
        for k in range(N_ADA):
            for b in range(bsz):
                mod_ref[k, b] = mod_acc[b:b + 1, k * d:(k + 1) * d]

        col = lax.broadcasted_iota(jnp.int32, (HEAD_DIM, MXU_DIM), 1)
        row = lax.broadcasted_iota(jnp.int32, (HEAD_DIM, MXU_DIM), 0)
        expand = jnp.where((col & (HEAD_DIM - 1)) == row, 1.0, 0.0).astype(bf16)
        head_of_row = lax.broadcasted_iota(jnp.int32, (MXU_DIM, MXU_DIM), 0) >> HEAD_SHIFT
        head_of_col = lax.broadcasted_iota(jnp.int32, (MXU_DIM, MXU_DIM), 1) >> HEAD_SHIFT
        for t in range(D_LRU // MXU_DIM):
            for g, w_ref in enumerate((gaw_ref, gxw_ref)):
                w4 = (0.5 * w_ref[t * MXU_DIM:(t + 1) * MXU_DIM, :]).astype(bf16)
                tiled = jnp.dot(w4, expand, preferred_element_type=f32)
                wg_ref[t, :, g * MXU_DIM:(g + 1) * MXU_DIM] = jnp.where(
                    head_of_row == head_of_col, tiled, 0.0).astype(bf16)

    last_step = pl.num_programs(0) - 1

    @pl.when(step + (n_slots - 1) < n_chunks)
    def _():
        x_copy(step + (n_slots - 1)).start()

    @pl.when(step >= 3)
    def _():
        o_copy(step - 3).wait()

    @pl.when(step < last_step)
    def _():
        x_copy(step).wait()
        run(mix=True)

    @pl.when(step == last_step)
    def _():
        run(mix=False)

    @pl.when(step >= 1)
    def _():
        o_copy(step - 1).start()

    @pl.when(step == last_step)
    def _():
        o_copy(step - 2).wait()
        o_copy(step - 1).wait()


def _head_pool_matrix():
    blocks = np.kron(np.eye(HEADS_PER_MXU_TILE), np.full((HEAD_DIM, HEAD_DIM), 1.0 / HEAD_DIM))
    return jnp.asarray(blocks, dtype=jnp.bfloat16)


def _const_spec(shape):
    if shape is None:
        return pl.BlockSpec(memory_space=pl.ANY)
    return pl.BlockSpec(shape, lambda s: (0,) * len(shape), pipeline_mode=pl.Buffered(1))


@jax.jit
def kernel(x, c, ada_w, ada_b, norm1_g, w_in, lru_conv_w, lru_conv_b, gate_a_w, gate_a_b, gate_x_w,
           gate_x_b, a_param, short_conv_w, lru_out_g, conv_out_g, w_out, norm2_g, w_mlp1, w_mlp2,
           final_g):
    bsz, seq, d = x.shape
    assert (d, seq % TIME_CHUNK, bsz) == (D_MODEL, 0, SUBLANES)
    assert ada_w.shape[0] == 1, "one layer"

    operands = [
        (c, (bsz, d)),
        (ada_w.reshape(d, N_ADA * d), None),
        (ada_b, (1, N_ADA * d)),
        (norm1_g, (1, d)),
        (w_in.reshape(d, D_IN), None),
        (lru_conv_w.reshape(4, D_LRU), (4, D_LRU)),
        (lru_conv_b, (1, D_LRU)),
        (gate_a_w.reshape(D_LRU, HEAD_DIM), (D_LRU, HEAD_DIM)),
        (gate_x_w.reshape(D_LRU, HEAD_DIM), (D_LRU, HEAD_DIM)),
        (gate_a_b, (1, D_LRU)),
        (gate_x_b, (1, D_LRU)),
        (a_param, (1, D_LRU)),
        (short_conv_w.reshape(3, D_CONV), (3, D_CONV)),
        (lru_out_g, (1, D_LRU)),
        (conv_out_g, (1, D_CONV)),
        (_head_pool_matrix(), (MXU_DIM, MXU_DIM)),
        (w_out.reshape(D_LRU + D_CONV, d), None),
        (norm2_g, (1, d)),
        (w_mlp1.reshape(d, D_FF), None),
        (w_mlp2.reshape(D_FF, d), None),
        (final_g.reshape(1, d), (1, d)),
    ]
    n_chunks = seq // TIME_CHUNK
    block = (bsz, TIME_CHUNK, d)
    rows = bsz * TIME_CHUNK
    return pl.pallas_call(
        _block_kernel,
        grid=(n_chunks + 1,),
        in_specs=[pl.BlockSpec(memory_space=pl.ANY)]
        + [_const_spec(shape) for _, shape in operands],
        out_specs=pl.BlockSpec(memory_space=pl.ANY),
        out_shape=jax.ShapeDtypeStruct(x.shape, x.dtype),
        scratch_shapes=[
            pltpu.VMEM((bsz, SUBLANES, D_LRU), jnp.float32),
            pltpu.VMEM((bsz, SUBLANES, D_CONV), jnp.float32),
            pltpu.VMEM((bsz, D_LRU), jnp.float32),
            pltpu.VMEM((N_SLABS, bsz * SCAN_PITCH, LANES), jnp.float32),
            pltpu.VMEM((N_SLABS, bsz * SCAN_PITCH, LANES), jnp.float32),
            pltpu.VMEM((N_SLABS, bsz * SCAN_PITCH, LANES), jnp.float32),
            pltpu.VMEM((rows, d), jnp.bfloat16),
            pltpu.VMEM(block, jnp.float32),
            pltpu.VMEM((rows, d), jnp.bfloat16),
            pltpu.VMEM((rows, D_FF), jnp.bfloat16),
            pltpu.VMEM((rows, d), jnp.float32),
            pltpu.VMEM((rows, D_LRU), jnp.bfloat16),
            pltpu.VMEM((rows, D_LRU), jnp.bfloat16),
            pltpu.VMEM((rows, D_CONV), jnp.bfloat16),
            pltpu.VMEM((rows, D_LRU + D_CONV), jnp.bfloat16),
            pltpu.VMEM((d, D_IN), jnp.bfloat16),
            pltpu.VMEM((D_LRU + D_CONV, d), jnp.bfloat16),
            pltpu.VMEM((d, D_FF), jnp.bfloat16),
            pltpu.VMEM((D_FF, d), jnp.bfloat16),
            pltpu.VMEM((STAGE_SLOTS, STAGE_ROWS, STAGE_COLS), jnp.float32),
            pltpu.SemaphoreType.DMA((STAGE_SLOTS,)),
            pltpu.VMEM((bsz, N_ADA * d), jnp.float32),
            pltpu.VMEM((N_ADA, bsz, 1, d), jnp.float32),
            pltpu.VMEM((D_LRU // MXU_DIM, MXU_DIM, 2 * MXU_DIM), jnp.bfloat16),
            pltpu.VMEM((X_SLOTS,) + block, jnp.float32),
            pltpu.SemaphoreType.DMA((X_SLOTS,)),
            pltpu.VMEM((2,) + block, jnp.float32),
            pltpu.SemaphoreType.DMA((2,)),
        ],
        compiler_params=pltpu.CompilerParams(
            dimension_semantics=("arbitrary",), vmem_limit_bytes=_vmem_limit_bytes(bsz, d)),
        name="hybrid_block",
    )(x, *[v for v, _ in operands])
```
